```python
import math
import jax, jax.numpy as jnp
from jax import lax
import numpy as np

D_MODEL = 1024
BATCH = 2
SEQ = 8192
DEPTH = 2
DEC_BATCH = 4
DEC_SEQ = 8192
PAST_LEN = 128

GRID_W = 64
HEAD_DIM = 64
N_DIFF_HEADS = 4
DIFF_V_DIM = 2 * HEAD_DIM
DIFF_WIDTH = N_DIFF_HEADS * DIFF_V_DIM
N_NA_HEADS = 8
NA_WIDTH = N_NA_HEADS * HEAD_DIM
MIX_WIDTH = DIFF_WIDTH + NA_WIDTH
IN_COLS = 3 * DIFF_WIDTH + 3 * NA_WIDTH
NA_MAX_KH = 8
NA_KW = 16
Q_BLOCK = 128
D_FF = 2816
N_EXPERTS = 8
TOP_K = 2
D_FF_EXPERT = 3584
N_DENSE = (DEPTH + 1) // 2
N_MOE = DEPTH // 2
RMS_EPS = 1e-5

kernel_name = "hymba_diffattn_natten_moe_encoder"


def rmsnorm(x, g):
    xf = x.astype(jnp.float32)
    y = xf * lax.rsqrt(jnp.mean(xf * xf, axis=-1, keepdims=True) + RMS_EPS)
    return (y * g.astype(jnp.float32)).astype(x.dtype)


def alibi_slopes(n_heads):
    h = jnp.arange(1, n_heads + 1, dtype=jnp.float32)
    return jnp.exp2(-8.0 * h / n_heads)


def diff_attention(q, k, v, lam, subln_g, layer_idx):
    B, S, H = q.shape[0], q.shape[1], q.shape[2]
    lambda_init = 0.8 - 0.6 * math.exp(-0.3 * layer_idx)
    lf = lam.astype(jnp.float32)
    lam_full = jnp.exp(jnp.sum(lf[0] * lf[1])) - jnp.exp(jnp.sum(lf[2] * lf[3])) + lambda_init
    qh = q.transpose(0, 2, 3, 1, 4)
    kh = k.transpose(0, 2, 3, 1, 4)
    vh = v.transpose(0, 2, 1, 3)
    nblk = S // Q_BLOCK
    qb = qh.reshape(B, H, 2, nblk, Q_BLOCK, HEAD_DIM).transpose(3, 0, 1, 2, 4, 5)
    slopes = alibi_slopes(H)
    kpos = jnp.arange(S, dtype=jnp.float32)
    scale = HEAD_DIM ** -0.5

    def one_block(args):
        qblk, start = args
        s = jnp.einsum('bhmqd,bhmkd->bhmqk', qblk, kh,
                       preferred_element_type=jnp.float32) * scale
        qpos = start + jnp.arange(Q_BLOCK, dtype=jnp.float32)
        dist = jnp.abs(qpos[:, None] - kpos[None, :])
        s = s - slopes[None, :, None, None, None] * dist[None, None, None]
        p = jax.nn.softmax(s, axis=-1)
        a = p[:, :, 0] - lam_full * p[:, :, 1]
        return jnp.einsum('bhqk,bhkd->bhqd', a.astype(vh.dtype), vh)

    starts = (jnp.arange(nblk, dtype=jnp.int32) * Q_BLOCK).astype(jnp.float32)
    o = lax.map(one_block, (qb, starts))
    o = o.transpose(1, 0, 3, 2, 4).reshape(B, S, H, DIFF_V_DIM)
    o = rmsnorm(o, subln_g) * (1.0 - lambda_init)
    return o.reshape(B, S, H * DIFF_V_DIM)


def neighbourhood_attention(q, k, v, rpb):
    B, S, H = q.shape[0], q.shape[1], q.shape[2]
    rows = S // GRID_W
    kh_ = min(NA_MAX_KH, rows)
    to_grid = lambda t: t.reshape(B, rows, GRID_W, H, HEAD_DIM).transpose(0, 3, 1, 2, 4)
    qg, kg, vg = to_grid(q), to_grid(k), to_grid(v)
    cols = jnp.arange(GRID_W)
    col_start = jnp.clip(cols - NA_KW // 2, 0, GRID_W - NA_KW)
    col_idx = col_start[:, None] + jnp.arange(NA_KW)[None, :]
    col_rel = col_idx - cols[:, None] + (NA_KW - 1)
    scale = HEAD_DIM ** -0.5
    rpb_f = rpb.astype(jnp.float32)

    def one_row(r):
        rs = jnp.clip(r - kh_ // 2, 0, rows - kh_)
        kr = lax.dynamic_slice_in_dim(kg, rs, kh_, axis=2)
        vr = lax.dynamic_slice_in_dim(vg, rs, kh_, axis=2)
        kc = kr[:, :, :, col_idx, :]
        vc = vr[:, :, :, col_idx, :]
        qr = lax.dynamic_index_in_dim(qg, r, axis=2, keepdims=False)
        s = jnp.einsum('bhqd,bhiqjd->bhqij', qr, kc,
                       preferred_element_type=jnp.float32) * scale
        row_rel = rs + jnp.arange(kh_) - r + (NA_MAX_KH - 1)
        bias = rpb_f[:, row_rel[None, :, None], col_rel[:, None, :]]
        s = s + bias[None]
        p = jax.nn.softmax(s.reshape(B, H, GRID_W, kh_ * NA_KW), axis=-1)
        p = p.reshape(B, H, GRID_W, kh_, NA_KW)
        return jnp.einsum('bhqij,bhiqjd->bhqd', p.astype(vc.dtype), vc)

    o = lax.map(one_row, jnp.arange(rows, dtype=jnp.int32))
    return o.transpose(1, 0, 3, 2, 4).reshape(B, S, H * HEAD_DIM)


def mixer(h, w_in, w_out, lam, subln_g, rpb, layer_idx):
    B, S, _ = h.shape
    proj = h @ w_in
    splits = np.cumsum([DIFF_WIDTH, DIFF_WIDTH, DIFF_WIDTH, NA_WIDTH, NA_WIDTH])
    dq, dk, dv, nq, nk, nv = jnp.split(proj, splits.tolist(), axis=-1)
    oa = diff_attention(dq.reshape(B, S, N_DIFF_HEADS, 2, HEAD_DIM),
                        dk.reshape(B, S, N_DIFF_HEADS, 2, HEAD_DIM),
                        dv.reshape(B, S, N_DIFF_HEADS, DIFF_V_DIM),
                        lam, subln_g, layer_idx)
    ob = neighbourhood_attention(nq.reshape(B, S, N_NA_HEADS, HEAD_DIM),
                                 nk.reshape(B, S, N_NA_HEADS, HEAD_DIM),
                                 nv.reshape(B, S, N_NA_HEADS, HEAD_DIM), rpb)
    return jnp.concatenate([oa, ob], axis=-1) @ w_out


def swiglu(x, w1, w3, w2):
    return (jax.nn.silu(x @ w1) * (x @ w3)) @ w2


def moe_swiglu(x, router, w1, w3, w2):
    B, S, D = x.shape
    xf = x.reshape(B * S, D)
    logits = jnp.matmul(xf, router, preferred_element_type=jnp.float32)
    top_v, top_i = lax.top_k(logits, TOP_K)
    g = jax.nn.softmax(top_v, axis=-1)
    gates = jnp.sum(jax.nn.one_hot(top_i, N_EXPERTS, dtype=jnp.float32) * g[..., None], axis=1)
    out = jnp.zeros_like(xf)
    for e in range(N_EXPERTS):
        out = out + gates[:, e:e + 1].astype(x.dtype) * swiglu(xf, w1[e], w3[e], w2[e])
    return out.reshape(B, S, D)


def trunk(x, w_in, w_out, norm_mix, norm_ffn, lambda_qk, subln_g, na_rpb,
          ffn_w1, ffn_w3, ffn_w2, moe_router, moe_w1, moe_w3, moe_w2, norm_final):
    for l in range(DEPTH):
        h = rmsnorm(x, norm_mix[l])
        x = x + mixer(h, w_in[l], w_out[l], lambda_qk[l], subln_g[l], na_rpb[l], l)
        h = rmsnorm(x, norm_ffn[l])
        i = l // 2
        if l % 2 == 0:
            x = x + swiglu(h, ffn_w1[i], ffn_w3[i], ffn_w2[i])
        else:
            x = x + moe_swiglu(h, moe_router[i], moe_w1[i], moe_w3[i], moe_w2[i])
    return rmsnorm(x, norm_final)


def setup_inputs(seed: int = 0) -> dict:
    key = jax.random.key(seed)
    ks = jax.random.split(key, 20)
    nrm = lambda k, shape, s: jax.random.normal(k, shape, dtype=jnp.float32) * s
    return {
        "x_prompt": nrm(ks[0], (BATCH, SEQ, D_MODEL), 1.0),
        "x_sample": nrm(ks[1], (DEC_BATCH, DEC_SEQ, D_MODEL), 1.0),
        "w_in": nrm(ks[2], (DEPTH, D_MODEL, IN_COLS), D_MODEL ** -0.5),
        "w_out": nrm(ks[3], (DEPTH, MIX_WIDTH, D_MODEL), MIX_WIDTH ** -0.5),
        "norm_mix": 1.0 + nrm(ks[4], (DEPTH, D_MODEL), 0.02),
        "norm_ffn": 1.0 + nrm(ks[5], (DEPTH, D_MODEL), 0.02),
        "lambda_qk": nrm(ks[6], (DEPTH, 4, HEAD_DIM), 0.1),
        "subln_g": 1.0 + nrm(ks[7], (DEPTH, DIFF_V_DIM), 0.02),
        "na_rpb": nrm(ks[8], (DEPTH, N_NA_HEADS, 2 * NA_MAX_KH - 1, 2 * NA_KW - 1), 0.02),
        "ffn_w1": nrm(ks[9], (N_DENSE, D_MODEL, D_FF), D_MODEL ** -0.5),
        "ffn_w3": nrm(ks[10], (N_DENSE, D_MODEL, D_FF), D_MODEL ** -0.5),
        "ffn_w2": nrm(ks[11], (N_DENSE, D_FF, D_MODEL), D_FF ** -0.5),
        "moe_router": nrm(ks[12], (N_MOE, D_MODEL, N_EXPERTS), D_MODEL ** -0.5),
        "moe_w1": nrm(ks[13], (N_MOE, N_EXPERTS, D_MODEL, D_FF_EXPERT), D_MODEL ** -0.5),
        "moe_w3": nrm(ks[14], (N_MOE, N_EXPERTS, D_MODEL, D_FF_EXPERT), D_MODEL ** -0.5),
        "moe_w2": nrm(ks[15], (N_MOE, N_EXPERTS, D_FF_EXPERT, D_MODEL), D_FF_EXPERT ** -0.5),
        "norm_final": 1.0 + nrm(ks[16], (D_MODEL,), 0.02),
    }


def reference(x_prompt, x_sample, w_in, w_out, norm_mix, norm_ffn, lambda_qk, subln_g, na_rpb,
              ffn_w1, ffn_w3, ffn_w2, moe_router, moe_w1, moe_w3, moe_w2, norm_final):
    y_prompt = trunk(x_prompt, w_in, w_out, norm_mix, norm_ffn, lambda_qk, subln_g, na_rpb,
                     ffn_w1, ffn_w3, ffn_w2, moe_router, moe_w1, moe_w3, moe_w2, norm_final)
    y_sample = trunk(x_sample, w_in, w_out, norm_mix, norm_ffn, lambda_qk, subln_g, na_rpb,
                     ffn_w1, ffn_w3, ffn_w2, moe_router, moe_w1, moe_w3, moe_w2, norm_final)
    return (y_prompt, y_sample)
```

```python
import functools
import math

import numpy as np
import jax
import jax.numpy as jnp
from jax import lax
from jax.experimental import pallas as pl
from jax.experimental.pallas import tpu as pltpu

F32 = jnp.float32
BF16 = jnp.bfloat16

RMS_EPS = 1e-5
HEAD_DIM = 64
N_DIFF_HEADS = 4
N_NA_HEADS = 8
GRID_W = 64
NA_KH = 8
NA_KW = 16
NA_GROUP = 8
NA_WIN = 16
TOP_K = 2
LANES = 128
NEG = -1e30
VMEM_LIMIT = 56 * 1024 * 1024

_NT = (((1,), (1,)), ((), ()))


def _rms(x, g):
    return x * lax.rsqrt(jnp.mean(x * x, axis=-1, keepdims=True) + RMS_EPS) * g


def _norm_proj_kernel(x_ref, g_ref, w_ref, o_ref, *, chunk):
    h = _rms(x_ref[...], g_ref[...]).astype(BF16)
    for c in range(o_ref.shape[1] // chunk):
        cols = slice(c * chunk, (c + 1) * chunk)
        o_ref[:, cols] = jnp.dot(h, w_ref[:, cols], preferred_element_type=F32).astype(BF16)


def norm_proj(x, g, w, tm=512):
    n, d = x.shape
    c = w.shape[1]
    return pl.pallas_call(
        functools.partial(_norm_proj_kernel, chunk=1024),
        grid=(n // tm,),
        in_specs=[pl.BlockSpec((tm, d), lambda i: (i, 0)),
                  pl.BlockSpec((1, d), lambda i: (0, 0)),
                  pl.BlockSpec((d, c), lambda i: (0, 0))],
        out_specs=pl.BlockSpec((tm, c), lambda i: (i, 0)),
        out_shape=jax.ShapeDtypeStruct((n, c), BF16),
        compiler_params=pltpu.CompilerParams(dimension_semantics=("parallel",),
                                             vmem_limit_bytes=VMEM_LIMIT),
        name="norm_proj")(x, g, w)


def _diff_attn_kernel(slopes_ref, q_ref, k_ref, v_ref, lam_ref, g_ref, o_ref,
                      k1_ref, k2_ref, va_ref, acc_ref, m_ref, *, t, lambda_init):
    h = pl.program_id(1)
    i = pl.program_id(2)
    n = k_ref.shape[1] // t
    slope = slopes_ref[h]
    lane = lax.broadcasted_iota(jnp.int32, (t, LANES), 1)
    low = lane < HEAD_DIM
    centred = (lax.broadcasted_iota(jnp.int32, (t, LANES), 0) - t // 2).astype(F32)

    @pl.when(i == 0)
    def _prepare_keys():
        e1 = jnp.where(lane == HEAD_DIM, -slope, jnp.where(lane == HEAD_DIM + 1, slope * centred, 0.0))
        e2 = jnp.where(lane == 0, -slope, jnp.where(lane == 1, slope * centred, 0.0))

        def body(j, carry):
            rows = pl.ds(pl.multiple_of(j * t, t), t)
            k = k_ref[0, rows, :].astype(F32)
            k1_ref[rows, :] = jnp.where(low, k, e1).astype(BF16)
            k2_ref[rows, :] = jnp.where(low, e2, k).astype(BF16)
            va_ref[rows, :LANES] = v_ref[0, rows, :]
            va_ref[rows, LANES:] = jnp.ones((t, LANES), BF16)
            return carry

        lax.fori_loop(0, n, body, 0)

    q = q_ref[0].astype(F32)
    qe1 = jnp.where(lane == HEAD_DIM, centred, jnp.where(lane == HEAD_DIM + 1, 1.0, 0.0))
    qe2 = jnp.where(lane == 0, centred, jnp.where(lane == 1, 1.0, 0.0))
    q1 = {s: jnp.where(low, q, s * qe1).astype(BF16) for s in (1, 0, -1)}
    q2 = {s: jnp.where(low, s * qe2, q).astype(BF16) for s in (1, 0, -1)}

    m_ref[...] = jnp.full(m_ref.shape, NEG, F32)
    acc_ref[...] = jnp.zeros(acc_ref.shape, F32)

    def tile(j, sign):
        rows = pl.ds(pl.multiple_of(j * t, t), t)
        va = va_ref[rows, :]
        shift = -slope * (jnp.abs(i - j) * t).astype(F32)
        for mi, (qm, km_ref) in enumerate(((q1[sign], k1_ref), (q2[sign], k2_ref))):
            s = lax.dot_general(qm, km_ref[rows, :], _NT, preferred_element_type=F32)
            if sign == 0:
                ri = lax.broadcasted_iota(jnp.int32, (t, t), 0)
                ci = lax.broadcasted_iota(jnp.int32, (t, t), 1)
                s = s - slope * jnp.abs(ri - ci).astype(F32)
            m_old = m_ref[mi]
            m_new = jnp.maximum(m_old, jnp.max(s, axis=1, keepdims=True) + shift)
            p = jnp.exp(s + (shift - m_new))
            acc_ref[mi] = (jnp.exp(m_old - m_new) * acc_ref[mi]
                           + jnp.dot(p.astype(BF16), va, preferred_element_type=F32))
            m_ref[mi] = m_new

    def before(j, carry):
        tile(j, 1)
        return carry

    def after(j, carry):
        tile(j, -1)
        return carry

    lax.fori_loop(0, i, before, 0)
    tile(i, 0)
    lax.fori_loop(i + 1, n, after, 0)

    a1 = acc_ref[0]
    a2 = acc_ref[1]
    lf = lam_ref[...]
    lam_full = (jnp.exp(jnp.sum(lf[0:1] * lf[1:2], axis=1, keepdims=True))
                - jnp.exp(jnp.sum(lf[2:3] * lf[3:4], axis=1, keepdims=True)) + lambda_init)
    o = a1[:, :LANES] / a1[:, LANES:] - lam_full * (a2[:, :LANES] / a2[:, LANES:])
    o_ref[0] = (_rms(o, g_ref[...]) * (1.0 - lambda_init)).astype(BF16)


def diff_attn(proj, lam, subln_g, layer_idx, t=512):
    b, s, _ = proj.shape
    nh = N_DIFF_HEADS
    assert (8 % nh) == 0 and t <= 512 and (t & (t - 1)) == 0
    slopes = jnp.asarray([2.0 ** (-8.0 * (h + 1) / nh) for h in range(nh)], F32)
    lambda_init = 0.8 - 0.6 * math.exp(-0.3 * layer_idx)
    grid_spec = pltpu.PrefetchScalarGridSpec(
        num_scalar_prefetch=1,
        grid=(b, nh, s // t),
        in_specs=[pl.BlockSpec((1, t, LANES), lambda bi, h, i, sl: (bi, i, h)),
                  pl.BlockSpec((1, s, LANES), lambda bi, h, i, sl: (bi, 0, nh + h)),
                  pl.BlockSpec((1, s, LANES), lambda bi, h, i, sl: (bi, 0, 2 * nh + h)),
                  pl.BlockSpec((4, HEAD_DIM), lambda bi, h, i, sl: (0, 0)),
                  pl.BlockSpec((1, 2 * HEAD_DIM), lambda bi, h, i, sl: (0, 0))],
        out_specs=pl.BlockSpec((1, t, LANES), lambda bi, h, i, sl: (bi, i, h)),
        scratch_shapes=[pltpu.VMEM((s, LANES), BF16),
                        pltpu.VMEM((s, LANES), BF16),
                        pltpu.VMEM((s, 2 * LANES), BF16),
                        pltpu.VMEM((2, t, 2 * LANES), F32),
                        pltpu.VMEM((2, t, 1), F32)])
    return pl.pallas_call(
        functools.partial(_diff_attn_kernel, t=t, lambda_init=lambda_init),
        grid_spec=grid_spec,
        out_shape=jax.ShapeDtypeStruct((b, s, nh * 2 * HEAD_DIM), BF16),
        compiler_params=pltpu.CompilerParams(
            dimension_semantics=("arbitrary", "arbitrary", "arbitrary"),
            vmem_limit_bytes=VMEM_LIMIT),
        name="diff_attn")(slopes, proj, proj, proj, lam, subln_g.reshape(1, -1))


def _na_window_start(group, rows):
    return jnp.clip(group * NA_GROUP - NA_KH // 2, 0, rows - NA_WIN)


def _na_bias_tables(rpb, rows):
    qr = np.arange(NA_GROUP)[:, None, None, None]
    c = np.arange(GRID_W)[None, :, None, None]
    kr = np.arange(NA_WIN)[None, None, :, None]
    kc = np.arange(GRID_W)[None, None, None, :]
    rho, chi, valid = [], [], []
    for r0 in (0, NA_GROUP, rows - NA_GROUP):
        ws = min(max(r0 - NA_KH // 2, 0), rows - NA_WIN)
        r = r0 + qr
        rs = np.clip(r - NA_KH // 2, 0, rows - NA_KH)
        krow = ws + kr
        cs = np.clip(c - NA_KW // 2, 0, GRID_W - NA_KW)
        ok = (krow >= rs) & (krow < rs + NA_KH) & (kc >= cs) & (kc < cs + NA_KW)
        shape = (NA_GROUP * GRID_W, NA_WIN * GRID_W)
        rho.append(np.broadcast_to(np.clip(krow - r + NA_KH - 1, 0, 2 * NA_KH - 2), ok.shape).reshape(shape))
        chi.append(np.broadcast_to(np.clip(kc - c + NA_KW - 1, 0, 2 * NA_KW - 2), ok.shape).reshape(shape))
        valid.append(ok.reshape(shape))
    rho, chi, valid = np.stack(rho), np.stack(chi), np.stack(valid)
    bias = rpb.astype(F32)[:, rho, chi]
    return jnp.where(valid[None], bias, NEG).transpose(1, 0, 2, 3)


def _na_attn_kernel(q_ref, k_ref, v_ref, t_ref, o_ref, *, rows):
    g = pl.program_id(2)
    start = pl.multiple_of(_na_window_start(g, rows) * GRID_W, 4 * GRID_W)
    win = pl.ds(start, NA_WIN * GRID_W)
    kw = k_ref[0, win, :]
    vw = v_ref[0, win, :]
    low = lax.broadcasted_iota(jnp.int32, kw.shape, 1) < HEAD_DIM
    zero = jnp.zeros_like(kw)
    q = q_ref[0]
    out = None
    for hi, keep in enumerate((low, ~low)):
        s = lax.dot_general(q, jnp.where(keep, kw, zero), _NT, preferred_element_type=F32) + t_ref[0, hi]
        p = jnp.exp(s - jnp.max(s, axis=1, keepdims=True))
        l = jnp.sum(p, axis=1, keepdims=True)
        o = jnp.dot(p.astype(BF16), jnp.where(keep, vw, zero), preferred_element_type=F32) / l
        out = o if out is None else out + o
    o_ref[0] = out.astype(BF16)


def na_attn(proj, rpb, col0):
    b, s, _ = proj.shape
    rows = s // GRID_W
    assert rows % NA_GROUP == 0 and rows >= 2 * NA_WIN
    groups = rows // NA_GROUP
    pairs = N_NA_HEADS // 2
    cb = col0 // LANES
    tables = _na_bias_tables(rpb, rows)
    tq = NA_GROUP * GRID_W

    def case(g):
        return jnp.where(g == 0, 0, jnp.where(g == groups - 1, 2, 1))

    return pl.pallas_call(
        functools.partial(_na_attn_kernel, rows=rows),
        grid=(pairs, b, groups),
        in_specs=[pl.BlockSpec((1, tq, LANES), lambda p, bi, g: (bi, g, cb + p)),
                  pl.BlockSpec((1, s, LANES), lambda p, bi, g: (bi, 0, cb + pairs + p)),
                  pl.BlockSpec((1, s, LANES), lambda p, bi, g: (bi, 0, cb + 2 * pairs + p)),
                  pl.BlockSpec((1, 2, tq, NA_WIN * GRID_W), lambda p, bi, g: (case(g), p, 0, 0))],
        out_specs=pl.BlockSpec((1, tq, LANES), lambda p, bi, g: (bi, g, p)),
        out_shape=jax.ShapeDtypeStruct((b, s, N_NA_HEADS * HEAD_DIM), BF16),
        compiler_params=pltpu.CompilerParams(
            dimension_semantics=("parallel", "parallel", "parallel"),
            vmem_limit_bytes=VMEM_LIMIT),
        name="na_attn")(proj, proj, proj, tables)


def _out_proj_kernel(x_ref, a_ref, b_ref, wa_ref, wb_ref, o_ref):
    o_ref[...] = (x_ref[...]
                  + jnp.dot(a_ref[...], wa_ref[...], preferred_element_type=F32)
                  + jnp.dot(b_ref[...], wb_ref[...], preferred_element_type=F32))


def out_proj(x, oa, ob, wa, wb, tm=512):
    n, d = x.shape
    return pl.pallas_call(
        _out_proj_kernel,
        grid=(n // tm,),
        in_specs=[pl.BlockSpec((tm, d), lambda i: (i, 0)),
                  pl.BlockSpec((tm, oa.shape[1]), lambda i: (i, 0)),
                  pl.BlockSpec((tm, ob.shape[1]), lambda i: (i, 0)),
                  pl.BlockSpec(wa.shape, lambda i: (0, 0)),
                  pl.BlockSpec(wb.shape, lambda i: (0, 0))],
        out_specs=pl.BlockSpec((tm, d), lambda i: (i, 0)),
        out_shape=jax.ShapeDtypeStruct((n, d), F32),
        compiler_params=pltpu.CompilerParams(dimension_semantics=("parallel",),
                                             vmem_limit_bytes=VMEM_LIMIT),
        name="out_proj")(x, oa, ob, wa, wb)


def _swiglu_act(a, b):
    return (a * jax.nn.sigmoid(a) * b).astype(BF16)


def _ffn_dense_kernel(x_ref, g_ref, w1_ref, w3_ref, w2_ref, gf_ref, o_ref, *, chunk, final_norm):
    x = x_ref[...]
    h = _rms(x, g_ref[...]).astype(BF16)
    y = x
    for c in range(w1_ref.shape[1] // chunk):
        cols = slice(c * chunk, (c + 1) * chunk)
        act = _swiglu_act(jnp.dot(h, w1_ref[:, cols], preferred_element_type=F32),
                          jnp.dot(h, w3_ref[:, cols], preferred_element_type=F32))
        y = y + jnp.dot(act, w2_ref[cols, :], preferred_element_type=F32)
    o_ref[...] = _rms(y, gf_ref[...]) if final_norm else y


def ffn_dense(x, g, w1, w3, w2, g_final, final_norm, tm=512):
    n, d = x.shape
    f = w1.shape[1]
    chunk = f // 2 if (f // 2) % LANES == 0 else f
    resident = dict(pipeline_mode=pl.Buffered(1))
    return pl.pallas_call(
        functools.partial(_ffn_dense_kernel, chunk=chunk, final_norm=final_norm),
        grid=(n // tm,),
        in_specs=[pl.BlockSpec((tm, d), lambda i: (i, 0)),
                  pl.BlockSpec((1, d), lambda i: (0, 0)),
                  pl.BlockSpec((d, f), lambda i: (0, 0), **resident),
                  pl.BlockSpec((d, f), lambda i: (0, 0), **resident),
                  pl.BlockSpec((f, d), lambda i: (0, 0), **resident),
                  pl.BlockSpec((1, d), lambda i: (0, 0))],
        out_specs=pl.BlockSpec((tm, d), lambda i: (i, 0)),
        out_shape=jax.ShapeDtypeStruct((n, d), F32),
        compiler_params=pltpu.CompilerParams(dimension_semantics=("parallel",),
                                             vmem_limit_bytes=VMEM_LIMIT),
        name="ffn_dense")(x, g, w1, w3, w2, g_final)


def _top2_gates(logits, n_experts):
    lane = lax.broadcasted_iota(jnp.int32, logits.shape, 1)
    lg = jnp.where(lane < n_experts, logits, -jnp.inf)
    v1 = jnp.max(lg, axis=1, keepdims=True)
    i1 = jnp.min(jnp.where(lg == v1, lane, LANES), axis=1, keepdims=True)
    lg2 = jnp.where(lane == i1, -jnp.inf, lg)
    v2 = jnp.max(lg2, axis=1, keepdims=True)
    i2 = jnp.min(jnp.where(lg2 == v2, lane, LANES), axis=1, keepdims=True)
    e2 = jnp.exp(v2 - v1)
    den = 1.0 + e2
    return jnp.where(lane == i1, 1.0 / den, 0.0) + jnp.where(lane == i2, e2 / den, 0.0)


def _moe_dense_kernel(x_ref, g_ref, r_ref, w1_ref, w3_ref, w2_ref, gf_ref, o_ref,
                      h_ref, gate_ref, acce_ref, acc_ref, *, n_experts, final_norm):
    e = pl.program_id(1)
    f = pl.program_id(2)
    nf = pl.num_programs(2)

    @pl.when((e == 0) & (f == 0))
    def _route():
        h = _rms(x_ref[...], g_ref[...]).astype(BF16)
        h_ref[...] = h
        gate_ref[...] = _top2_gates(jnp.dot(h, r_ref[...], preferred_element_type=F32), n_experts)
        acc_ref[...] = jnp.zeros(acc_ref.shape, F32)

    h = h_ref[...]
    act = _swiglu_act(jnp.dot(h, w1_ref[0], preferred_element_type=F32),
                      jnp.dot(h, w3_ref[0], preferred_element_type=F32))
    part = jnp.dot(act, w2_ref[0], preferred_element_type=F32)

    @pl.when(f == 0)
    def _first():
        acce_ref[...] = part

    @pl.when(f > 0)
    def _rest():
        acce_ref[...] += part

    @pl.when(f == nf - 1)
    def _expert_done():
        lane = lax.broadcasted_iota(jnp.int32, gate_ref.shape, 1)
        ge = jnp.sum(jnp.where(lane == e, gate_ref[...], 0.0), axis=1, keepdims=True)
        acc_ref[...] += ge * acce_ref[...]

    @pl.when((f == nf - 1) & (e == n_experts - 1))
    def _finish():
        y = x_ref[...] + acc_ref[...]
        o_ref[...] = _rms(y, gf_ref[...]) if final_norm else y


def moe_dense(x, g, router, w1, w3, w2, g_final, final_norm, tm=1024, tf=512):
    n, d = x.shape
    ne, _, f = w1.shape
    tm = min(tm, n)
    tf = tf if f % tf == 0 else f
    return pl.pallas_call(
        functools.partial(_moe_dense_kernel, n_experts=ne, final_norm=final_norm),
        grid=(n // tm, ne, f // tf),
        in_specs=[pl.BlockSpec((tm, d), lambda i, e, j: (i, 0)),
                  pl.BlockSpec((1, d), lambda i, e, j: (0, 0)),
                  pl.BlockSpec((d, LANES), lambda i, e, j: (0, 0)),
                  pl.BlockSpec((1, d, tf), lambda i, e, j: (e, 0, j)),
                  pl.BlockSpec((1, d, tf), lambda i, e, j: (e, 0, j)),
                  pl.BlockSpec((1, tf, d), lambda i, e, j: (e, j, 0)),
                  pl.BlockSpec((1, d), lambda i, e, j: (0, 0))],
        out_specs=pl.BlockSpec((tm, d), lambda i, e, j: (i, 0)),
        out_shape=jax.ShapeDtypeStruct((n, d), F32),
        scratch_shapes=[pltpu.VMEM((tm, d), BF16),
                        pltpu.VMEM((tm, LANES), F32),
                        pltpu.VMEM((tm, d), F32),
                        pltpu.VMEM((tm, d), F32)],
        compiler_params=pltpu.CompilerParams(
            dimension_semantics=("parallel", "arbitrary", "arbitrary"),
            vmem_limit_bytes=VMEM_LIMIT),
        name="moe_dense")(x, g, router, w1, w3, w2, g_final)


def _trunk(x, w_in, w_out, norm_mix, norm_ffn, lambda_qk, subln_g, na_rpb,
           ffn_w1, ffn_w3, ffn_w2, moe_router, moe_w1, moe_w3, moe_w2, norm_final):
    b, s, d = x.shape
    depth = w_in.shape[0]
    diff_w = N_DIFF_HEADS * 2 * HEAD_DIM
    na_w = N_NA_HEADS * HEAD_DIM
    scale = HEAD_DIM ** -0.5
    col_scale = jnp.ones((w_in.shape[2],), F32)
    col_scale = col_scale.at[:diff_w].set(scale).at[3 * diff_w:3 * diff_w + na_w].set(scale)
    x = x.reshape(b * s, d)
    for l in range(depth):
        proj = norm_proj(x, norm_mix[l].reshape(1, d), (w_in[l] * col_scale).astype(BF16))
        proj = proj.reshape(b, s, -1)
        oa = diff_attn(proj, lambda_qk[l], subln_g[l], l).reshape(b * s, diff_w)
        ob = na_attn(proj, na_rpb[l], 3 * diff_w).reshape(b * s, na_w)
        wo = w_out[l].astype(BF16)
        x = out_proj(x, oa, ob, wo[:diff_w], wo[diff_w:])
        last = l == depth - 1
        i = l // 2
        if l % 2 == 0:
            x = ffn_dense(x, norm_ffn[l].reshape(1, d), ffn_w1[i].astype(BF16),
                          ffn_w3[i].astype(BF16), ffn_w2[i].astype(BF16),
                          norm_final.reshape(1, d), final_norm=last)
        else:
            ne = moe_router.shape[2]
            router = jnp.zeros((d, LANES), BF16).at[:, :ne].set(moe_router[i].astype(BF16))
            x = moe_dense(x, norm_ffn[l].reshape(1, d), router, moe_w1[i].astype(BF16),
                          moe_w3[i].astype(BF16), moe_w2[i].astype(BF16),
                          norm_final.reshape(1, d), final_norm=last)
    return x.reshape(b, s, d)


def kernel(x_prompt, x_sample, w_in, w_out, norm_mix, norm_ffn, lambda_qk, subln_g, na_rpb,
           ffn_w1, ffn_w3, ffn_w2, moe_router, moe_w1, moe_w3, moe_w2, norm_final):
    assert x_prompt.shape[1:] == x_sample.shape[1:]
    nb = x_prompt.shape[0]
    y = _trunk(jnp.concatenate([x_prompt, x_sample], axis=0),
               w_in, w_out, norm_mix, norm_ffn, lambda_qk, subln_g, na_rpb,
               ffn_w1, ffn_w3, ffn_w2, moe_router, moe_w1, moe_w3, moe_w2, norm_final)
    return (y[:nb], y[nb:])
```

```python
import functools
import math

import numpy as np
import jax
import jax.numpy as jnp
from jax import lax
from jax.experimental import pallas as pl
from jax.experimental.pallas import tpu as pltpu

F32 = jnp.float32
BF16 = jnp.bfloat16

RMS_EPS = 1e-5
HEAD_DIM = 64
N_DIFF_HEADS = 4
N_NA_HEADS = 8
GRID_W = 64
NA_KH = 8
NA_KW = 16
NA_GROUP = 8
NA_WIN = 16
TOP_K = 2
LANES = 128
NEG = -1e30
VMEM_LIMIT = 56 * 1024 * 1024

_NT = (((1,), (1,)), ((), ()))


def _rms(x, g):
    return x * lax.rsqrt(jnp.mean(x * x, axis=-1, keepdims=True) + RMS_EPS) * g


def _norm_proj_kernel(x_ref, g_ref, w_ref, o_ref, *, chunk):
    h = _rms(x_ref[...], g_ref[...]).astype(BF16)
    for c in range(o_ref.shape[1] // chunk):
        cols = slice(c * chunk, (c + 1) * chunk)
        o_ref[:, cols] = jnp.dot(h, w_ref[:, cols], preferred_element_type=F32).astype(BF16)


def norm_proj(x, g, w, tm=512):
    n, d = x.shape
    c = w.shape[1]
    return pl.pallas_call(
        functools.partial(_norm_proj_kernel, chunk=1024),
        grid=(n // tm,),
        in_specs=[pl.BlockSpec((tm, d), lambda i: (i, 0)),
                  pl.BlockSpec((1, d), lambda i: (0, 0)),
                  pl.BlockSpec((d, c), lambda i: (0, 0))],
        out_specs=pl.BlockSpec((tm, c), lambda i: (i, 0)),
        out_shape=jax.ShapeDtypeStruct((n, c), BF16),
        compiler_params=pltpu.CompilerParams(dimension_semantics=("parallel",),
                                             vmem_limit_bytes=VMEM_LIMIT),
        name="norm_proj")(x, g, w)


SCORE_CAP = 40.0
EXP_ZERO = 88.0


def _diff_attn_kernel(slopes_ref, q_ref, k_ref, v_ref, lam_ref, g_ref, o_ref,
                      k1_ref, k2_ref, va_ref, qv_ref, acc_ref, m_ref, kn_ref, *, t, lambda_init):
    h = pl.program_id(1)
    i = pl.program_id(2)
    n = k_ref.shape[1] // t
    slope = slopes_ref[h]
    lane = lax.broadcasted_iota(jnp.int32, (t, LANES), 1)
    low = lane < HEAD_DIM
    centred = (lax.broadcasted_iota(jnp.int32, (t, LANES), 0) - t // 2).astype(F32)

    def extras(base, a, b, c, d):
        return jnp.where(lane == base, a, jnp.where(lane == base + 1, b,
                         jnp.where(lane == base + 2, c, jnp.where(lane == base + 3, d, 0.0))))

    @pl.when(i == 0)
    def _prepare_keys():
        def body(j, norms):
            rows = pl.ds(pl.multiple_of(j * t, t), t)
            k = k_ref[0, rows, :].astype(F32)
            base = slope * (j * t).astype(F32)
            k1_ref[rows, :] = jnp.where(low, k, extras(HEAD_DIM, -slope, slope * centred, 1.0, base)).astype(BF16)
            k2_ref[rows, :] = jnp.where(low, extras(0, -slope, slope * centred, 1.0, base), k).astype(BF16)
            va_ref[rows, :LANES] = v_ref[0, rows, :]
            va_ref[rows, LANES:] = jnp.ones((t, LANES), BF16)
            k2 = k * k
            n1 = jnp.sum(jnp.where(low, k2, 0.0), axis=1, keepdims=True)
            n2 = jnp.sum(jnp.where(low, 0.0, k2), axis=1, keepdims=True)
            return jnp.maximum(norms[0], n1), jnp.maximum(norms[1], n2)

        zero = jnp.zeros((t, 1), F32)
        n1, n2 = lax.fori_loop(0, n, body, (zero, zero))
        kn_ref[0] = jnp.max(n1)
        kn_ref[1] = jnp.max(n2)

    q = q_ref[0].astype(F32)
    qbase = -slope * (i * t).astype(F32)
    for si, sign in enumerate((1.0, 0.0, -1.0)):
        qv_ref[0, si] = jnp.where(low, q, sign * extras(HEAD_DIM, centred, 1.0, qbase, 1.0)).astype(BF16)
        qv_ref[1, si] = jnp.where(low, sign * extras(0, centred, 1.0, qbase, 1.0), q).astype(BF16)
    before_q, diag_q, after_q = 0, 1, 2

    q2 = q * q
    u2 = jnp.maximum(jnp.sum(jnp.where(low, q2, 0.0), axis=1, keepdims=True) * kn_ref[0],
                     jnp.sum(jnp.where(low, 0.0, q2), axis=1, keepdims=True) * kn_ref[1])
    u = 1.01 * jnp.sqrt(u2)
    unshifted = jnp.max(u) <= SCORE_CAP
    reach = jnp.max(jnp.floor((EXP_ZERO + 2.0 * u) / (slope * t)) + 1.0).astype(jnp.int32)
    reach = jnp.where(unshifted, reach, n)
    j_lo = jnp.maximum(i - reach, 0)
    j_hi = jnp.minimum(i + reach, n - 1)

    acc_ref[...] = jnp.zeros(acc_ref.shape, F32)

    def scores(mi, qi, rows):
        km_ref = k1_ref if mi == 0 else k2_ref
        s = lax.dot_general(qv_ref[mi, qi], km_ref[rows, :], _NT, preferred_element_type=F32)
        if qi == diag_q:
            ri = lax.broadcasted_iota(jnp.int32, (t, t), 0)
            ci = lax.broadcasted_iota(jnp.int32, (t, t), 1)
            s = s - slope * jnp.abs(ri - ci).astype(F32)
        return s

    def plain_tile(j, qi, width=1):
        rows = pl.ds(pl.multiple_of(j * t, t), width * t)
        va = va_ref[rows, :]
        for mi in range(2):
            p = jnp.exp(scores(mi, qi, rows)).astype(BF16)
            acc_ref[mi] += jnp.dot(p, va, preferred_element_type=F32)

    def online_tile(j, qi, width=1):
        rows = pl.ds(pl.multiple_of(j * t, t), width * t)
        va = va_ref[rows, :]
        for mi in range(2):
            s = scores(mi, qi, rows)
            m_old = m_ref[mi]
            m_new = jnp.maximum(m_old, jnp.max(s, axis=1, keepdims=True))
            p = jnp.exp(s - m_new)
            acc_ref[mi] = (jnp.exp(m_old - m_new) * acc_ref[mi]
                           + jnp.dot(p.astype(BF16), va, preferred_element_type=F32))
            m_ref[mi] = m_new

    def walk(tile, width):
        def run(lo, hi, qi):
            odd = (hi - lo) % width
            for r in range(width - 1):
                @pl.when(r < odd)
                def _single():
                    tile(lo + r, qi)

            def body(c, carry):
                tile(lo + odd + c * width, qi, width)
                return carry
            lax.fori_loop(0, (hi - lo) // width, body, 0)

        run(j_lo, i, before_q)
        tile(i, diag_q)
        run(i + 1, j_hi + 1, after_q)

    @pl.when(unshifted)
    def _plain():
        walk(plain_tile, 2)

    @pl.when(jnp.logical_not(unshifted))
    def _online():
        m_ref[...] = jnp.full(m_ref.shape, NEG, F32)
        walk(online_tile, 1)

    a1 = acc_ref[0]
    a2 = acc_ref[1]
    lf = lam_ref[...]
    lam_full = (jnp.exp(jnp.sum(lf[0:1] * lf[1:2], axis=1, keepdims=True))
                - jnp.exp(jnp.sum(lf[2:3] * lf[3:4], axis=1, keepdims=True)) + lambda_init)
    o = a1[:, :LANES] / a1[:, LANES:] - lam_full * (a2[:, :LANES] / a2[:, LANES:])
    o_ref[0] = (_rms(o, g_ref[...]) * (1.0 - lambda_init)).astype(BF16)


def diff_attn(proj, lam, subln_g, layer_idx, t=512):
    b, s, _ = proj.shape
    nh = N_DIFF_HEADS
    assert (8 % nh) == 0 and t <= 512 and (t & (t - 1)) == 0
    slopes = jnp.asarray([2.0 ** (-8.0 * (h + 1) / nh) for h in range(nh)], F32)
    lambda_init = 0.8 - 0.6 * math.exp(-0.3 * layer_idx)
    grid_spec = pltpu.PrefetchScalarGridSpec(
        num_scalar_prefetch=1,
        grid=(b, nh, s // t),
        in_specs=[pl.BlockSpec((1, t, LANES), lambda bi, h, i, sl: (bi, i, h)),
                  pl.BlockSpec((1, s, LANES), lambda bi, h, i, sl: (bi, 0, nh + h)),
                  pl.BlockSpec((1, s, LANES), lambda bi, h, i, sl: (bi, 0, 2 * nh + h)),
                  pl.BlockSpec((4, HEAD_DIM), lambda bi, h, i, sl: (0, 0)),
                  pl.BlockSpec((1, 2 * HEAD_DIM), lambda bi, h, i, sl: (0, 0))],
        out_specs=pl.BlockSpec((1, t, LANES), lambda bi, h, i, sl: (bi, i, h)),
        scratch_shapes=[pltpu.VMEM((s, LANES), BF16),
                        pltpu.VMEM((s, LANES), BF16),
                        pltpu.VMEM((s, 2 * LANES), BF16),
                        pltpu.VMEM((2, 3, t, LANES), BF16),
                        pltpu.VMEM((2, t, 2 * LANES), F32),
                        pltpu.VMEM((2, t, 1), F32),
                        pltpu.SMEM((2,), F32)])
    return pl.pallas_call(
        functools.partial(_diff_attn_kernel, t=t, lambda_init=lambda_init),
        grid_spec=grid_spec,
        out_shape=jax.ShapeDtypeStruct((b, s, nh * 2 * HEAD_DIM), BF16),
        compiler_params=pltpu.CompilerParams(
            dimension_semantics=("arbitrary", "arbitrary", "arbitrary"),
            vmem_limit_bytes=VMEM_LIMIT),
        name="diff_attn")(slopes, proj, proj, proj, lam, subln_g.reshape(1, -1))


def _na_window_start(group, rows):
    return jnp.clip(group * NA_GROUP - NA_KH // 2, 0, rows - NA_WIN)


def _na_bias_tables(rpb, rows):
    n_rho, n_chi = 2 * NA_KH - 1, 2 * NA_KW - 1
    c = np.arange(GRID_W)[:, None]
    kc = np.arange(GRID_W)[None, :]
    cs = np.clip(c - NA_KW // 2, 0, GRID_W - NA_KW)
    col_ok = (kc >= cs) & (kc < cs + NA_KW)
    col_sel = ((kc - c + NA_KW - 1)[..., None] == np.arange(n_chi)) & col_ok[..., None]
    qr = np.arange(NA_GROUP)[:, None]
    kr = np.arange(NA_WIN)[None, :]
    row_sel, row_ok = [], []
    for r0 in (0, NA_GROUP, rows - NA_GROUP):
        ws = min(max(r0 - NA_KH // 2, 0), rows - NA_WIN)
        r = r0 + qr
        rs = np.clip(r - NA_KH // 2, 0, rows - NA_KH)
        ok = (ws + kr >= rs) & (ws + kr < rs + NA_KH)
        row_ok.append(ok)
        row_sel.append(((ws + kr - r + NA_KH - 1)[..., None] == np.arange(n_rho)) & ok[..., None])
    row_sel, row_ok = np.stack(row_sel), np.stack(row_ok)
    by_col = jnp.einsum("hrd,ckd->hrck", rpb.astype(F32), col_sel.astype(np.float32), precision="highest")
    bias = jnp.einsum("hrck,gqnr->ghqcnk", by_col, row_sel.astype(np.float32), precision="highest")
    valid = row_ok[:, None, :, None, :, None] & col_ok[None, None, None, :, None, :]
    shape = (3, rpb.shape[0], NA_GROUP * GRID_W, NA_WIN * GRID_W)
    return jnp.where(valid, bias, NEG).reshape(shape)


def _na_attn_kernel(q_ref, k_ref, v_ref, t_ref, o_ref, *, rows):
    g = pl.program_id(2)
    start = pl.multiple_of(_na_window_start(g, rows) * GRID_W, 4 * GRID_W)
    win = pl.ds(start, NA_WIN * GRID_W)
    kw = k_ref[0, win, :]
    vw = v_ref[0, win, :]
    low = lax.broadcasted_iota(jnp.int32, kw.shape, 1) < HEAD_DIM
    zero = jnp.zeros_like(kw)
    q = q_ref[0]
    out = None
    for hi, keep in enumerate((low, ~low)):
        s = lax.dot_general(q, jnp.where(keep, kw, zero), _NT, preferred_element_type=F32) + t_ref[0, hi]
        p = jnp.exp(s - jnp.max(s, axis=1, keepdims=True))
        l = jnp.sum(p, axis=1, keepdims=True)
        o = jnp.dot(p.astype(BF16), jnp.where(keep, vw, zero), preferred_element_type=F32) / l
        out = o if out is None else out + o
    o_ref[0] = out.astype(BF16)


def na_attn(proj, rpb, col0):
    b, s, _ = proj.shape
    rows = s // GRID_W
    assert rows % NA_GROUP == 0 and rows >= 2 * NA_WIN
    groups = rows // NA_GROUP
    pairs = N_NA_HEADS // 2
    cb = col0 // LANES
    tables = _na_bias_tables(rpb, rows)
    tq = NA_GROUP * GRID_W

    def case(g):
        return jnp.where(g == 0, 0, jnp.where(g == groups - 1, 2, 1))

    return pl.pallas_call(
        functools.partial(_na_attn_kernel, rows=rows),
        grid=(pairs, b, groups),
        in_specs=[pl.BlockSpec((1, tq, LANES), lambda p, bi, g: (bi, g, cb + p)),
                  pl.BlockSpec((1, s, LANES), lambda p, bi, g: (bi, 0, cb + pairs + p)),
                  pl.BlockSpec((1, s, LANES), lambda p, bi, g: (bi, 0, cb + 2 * pairs + p)),
                  pl.BlockSpec((1, 2, tq, NA_WIN * GRID_W), lambda p, bi, g: (case(g), p, 0, 0))],
        out_specs=pl.BlockSpec((1, tq, LANES), lambda p, bi, g: (bi, g, p)),
        out_shape=jax.ShapeDtypeStruct((b, s, N_NA_HEADS * HEAD_DIM), BF16),
        compiler_params=pltpu.CompilerParams(
            dimension_semantics=("parallel", "parallel", "parallel"),
            vmem_limit_bytes=VMEM_LIMIT),
        name="na_attn")(proj, proj, proj, tables)


def _out_proj_kernel(x_ref, a_ref, b_ref, wa_ref, wb_ref, o_ref):
    o_ref[...] = (x_ref[...]
                  + jnp.dot(a_ref[...], wa_ref[...], preferred_element_type=F32)
                  + jnp.dot(b_ref[...], wb_ref[...], preferred_element_type=F32))


def out_proj(x, oa, ob, wa, wb, tm=512):
    n, d = x.shape
    return pl.pallas_call(
        _out_proj_kernel,
        grid=(n // tm,),
        in_specs=[pl.BlockSpec((tm, d), lambda i: (i, 0)),
                  pl.BlockSpec((tm, oa.shape[1]), lambda i: (i, 0)),
                  pl.BlockSpec((tm, ob.shape[1]), lambda i: (i, 0)),
                  pl.BlockSpec(wa.shape, lambda i: (0, 0)),
                  pl.BlockSpec(wb.shape, lambda i: (0, 0))],
        out_specs=pl.BlockSpec((tm, d), lambda i: (i, 0)),
        out_shape=jax.ShapeDtypeStruct((n, d), F32),
        compiler_params=pltpu.CompilerParams(dimension_semantics=("parallel",),
                                             vmem_limit_bytes=VMEM_LIMIT),
        name="out_proj")(x, oa, ob, wa, wb)


def _swiglu_act(a, b):
    return (a * jax.nn.sigmoid(a) * b).astype(BF16)


def _ffn_dense_kernel(x_ref, g_ref, w1_ref, w3_ref, w2_ref, gf_ref, o_ref, *, chunk, final_norm):
    x = x_ref[...]
    h = _rms(x, g_ref[...]).astype(BF16)
    y = x
    for c in range(w1_ref.shape[1] // chunk):
        cols = slice(c * chunk, (c + 1) * chunk)
        act = _swiglu_act(jnp.dot(h, w1_ref[:, cols], preferred_element_type=F32),
                          jnp.dot(h, w3_ref[:, cols], preferred_element_type=F32))
        y = y + jnp.dot(act, w2_ref[cols, :], preferred_element_type=F32)
    o_ref[...] = _rms(y, gf_ref[...]) if final_norm else y


def ffn_dense(x, g, w1, w3, w2, g_final, final_norm, tm=512):
    n, d = x.shape
    f = w1.shape[1]
    chunk = f // 2 if (f // 2) % LANES == 0 else f
    resident = dict(pipeline_mode=pl.Buffered(1))
    return pl.pallas_call(
        functools.partial(_ffn_dense_kernel, chunk=chunk, final_norm=final_norm),
        grid=(n // tm,),
        in_specs=[pl.BlockSpec((tm, d), lambda i: (i, 0)),
                  pl.BlockSpec((1, d), lambda i: (0, 0)),
                  pl.BlockSpec((d, f), lambda i: (0, 0), **resident),
                  pl.BlockSpec((d, f), lambda i: (0, 0), **resident),
                  pl.BlockSpec((f, d), lambda i: (0, 0), **resident),
                  pl.BlockSpec((1, d), lambda i: (0, 0))],
        out_specs=pl.BlockSpec((tm, d), lambda i: (i, 0)),
        out_shape=jax.ShapeDtypeStruct((n, d), F32),
        compiler_params=pltpu.CompilerParams(dimension_semantics=("parallel",),
                                             vmem_limit_bytes=VMEM_LIMIT),
        name="ffn_dense")(x, g, w1, w3, w2, g_final)


def _top2_gates(logits, n_experts):
    lane = lax.broadcasted_iota(jnp.int32, logits.shape, 1)
    lg = jnp.where(lane < n_experts, logits, -jnp.inf)
    v1 = jnp.max(lg, axis=1, keepdims=True)
    i1 = jnp.min(jnp.where(lg == v1, lane, LANES), axis=1, keepdims=True)
    lg2 = jnp.where(lane == i1, -jnp.inf, lg)
    v2 = jnp.max(lg2, axis=1, keepdims=True)
    i2 = jnp.min(jnp.where(lg2 == v2, lane, LANES), axis=1, keepdims=True)
    e2 = jnp.exp(v2 - v1)
    den = 1.0 + e2
    return jnp.where(lane == i1, 1.0 / den, 0.0) + jnp.where(lane == i2, e2 / den, 0.0)


def _moe_dense_kernel(x_ref, g_ref, r_ref, w1_ref, w3_ref, w2_ref, gf_ref, o_ref,
                      h_ref, gate_ref, acce_ref, acc_ref, *, n_experts, final_norm):
    e = pl.program_id(1)
    f = pl.program_id(2)
    nf = pl.num_programs(2)

    @pl.when((e == 0) & (f == 0))
    def _route():
        h = _rms(x_ref[...], g_ref[...]).astype(BF16)
        h_ref[...] = h
        gate_ref[...] = _top2_gates(jnp.dot(h, r_ref[...], preferred_element_type=F32), n_experts)
        acc_ref[...] = jnp.zeros(acc_ref.shape, F32)

    h = h_ref[...]
    act = _swiglu_act(jnp.dot(h, w1_ref[0], preferred_element_type=F32),
                      jnp.dot(h, w3_ref[0], preferred_element_type=F32))
    part = jnp.dot(act, w2_ref[0], preferred_element_type=F32)

    @pl.when(f == 0)
    def _first():
        acce_ref[...] = part

    @pl.when(f > 0)
    def _rest():
        acce_ref[...] += part

    @pl.when(f == nf - 1)
    def _expert_done():
        lane = lax.broadcasted_iota(jnp.int32, gate_ref.shape, 1)
        ge = jnp.sum(jnp.where(lane == e, gate_ref[...], 0.0), axis=1, keepdims=True)
        acc_ref[...] += ge * acce_ref[...]

    @pl.when((f == nf - 1) & (e == n_experts - 1))
    def _finish():
        y = x_ref[...] + acc_ref[...]
        o_ref[...] = _rms(y, gf_ref[...]) if final_norm else y


def moe_dense(x, g, router, w1, w3, w2, g_final, final_norm, tm=1024, tf=512):
    n, d = x.shape
    ne, _, f = w1.shape
    tm = min(tm, n)
    tf = tf if f % tf == 0 else f
    return pl.pallas_call(
        functools.partial(_moe_dense_kernel, n_experts=ne, final_norm=final_norm),
        grid=(n // tm, ne, f // tf),
        in_specs=[pl.BlockSpec((tm, d), lambda i, e, j: (i, 0)),
                  pl.BlockSpec((1, d), lambda i, e, j: (0, 0)),
                  pl.BlockSpec((d, LANES), lambda i, e, j: (0, 0)),
                  pl.BlockSpec((1, d, tf), lambda i, e, j: (e, 0, j)),
                  pl.BlockSpec((1, d, tf), lambda i, e, j: (e, 0, j)),
                  pl.BlockSpec((1, tf, d), lambda i, e, j: (e, j, 0)),
                  pl.BlockSpec((1, d), lambda i, e, j: (0, 0))],
        out_specs=pl.BlockSpec((tm, d), lambda i, e, j: (i, 0)),
        out_shape=jax.ShapeDtypeStruct((n, d), F32),
        scratch_shapes=[pltpu.VMEM((tm, d), BF16),
                        pltpu.VMEM((tm, LANES), F32),
                        pltpu.VMEM((tm, d), F32),
                        pltpu.VMEM((tm, d), F32)],
        compiler_params=pltpu.CompilerParams(
            dimension_semantics=("parallel", "arbitrary", "arbitrary"),
            vmem_limit_bytes=VMEM_LIMIT),
        name="moe_dense")(x, g, router, w1, w3, w2, g_final)


def _trunk(x, w_in, w_out, norm_mix, norm_ffn, lambda_qk, subln_g, na_rpb,
           ffn_w1, ffn_w3, ffn_w2, moe_router, moe_w1, moe_w3, moe_w2, norm_final):
    b, s, d = x.shape
    depth = w_in.shape[0]
    diff_w = N_DIFF_HEADS * 2 * HEAD_DIM
    na_w = N_NA_HEADS * HEAD_DIM
    scale = HEAD_DIM ** -0.5
    col_scale = jnp.ones((w_in.shape[2],), F32)
    col_scale = col_scale.at[:diff_w].set(scale).at[3 * diff_w:3 * diff_w + na_w].set(scale)
    x = x.reshape(b * s, d)
    for l in range(depth):
        proj = norm_proj(x, norm_mix[l].reshape(1, d), (w_in[l] * col_scale).astype(BF16))
        proj = proj.reshape(b, s, -1)
        oa = diff_attn(proj, lambda_qk[l], subln_g[l], l).reshape(b * s, diff_w)
        ob = na_attn(proj, na_rpb[l], 3 * diff_w).reshape(b * s, na_w)
        wo = w_out[l].astype(BF16)
        x = out_proj(x, oa, ob, wo[:diff_w], wo[diff_w:])
        last = l == depth - 1
        i = l // 2
        if l % 2 == 0:
            x = ffn_dense(x, norm_ffn[l].reshape(1, d), ffn_w1[i].astype(BF16),
                          ffn_w3[i].astype(BF16), ffn_w2[i].astype(BF16),
                          norm_final.reshape(1, d), final_norm=last)
        else:
            ne = moe_router.shape[2]
            router = jnp.zeros((d, LANES), BF16).at[:, :ne].set(moe_router[i].astype(BF16))
            x = moe_dense(x, norm_ffn[l].reshape(1, d), router, moe_w1[i].astype(BF16),
                          moe_w3[i].astype(BF16), moe_w2[i].astype(BF16),
                          norm_final.reshape(1, d), final_norm=last)
    return x.reshape(b, s, d)


def kernel(x_prompt, x_sample, w_in, w_out, norm_mix, norm_ffn, lambda_qk, subln_g, na_rpb,
           ffn_w1, ffn_w3, ffn_w2, moe_router, moe_w1, moe_w3, moe_w2, norm_final):
    assert x_prompt.shape[1:] == x_sample.shape[1:]
    nb = x_prompt.shape[0]
    y = _trunk(jnp.concatenate([x_prompt, x_sample], axis=0),
               w_in, w_out, norm_mix, norm_ffn, lambda_qk, subln_g, na_rpb,
               ffn_w1, ffn_w3, ffn_w2, moe_router, moe_w1, moe_w3, moe_w2, norm_final)
    return (y[:nb], y[nb:])
```

```python
import functools
import math

import numpy as np
import jax
import jax.numpy as jnp
from jax import lax
from jax.experimental import pallas as pl
from jax.experimental.pallas import tpu as pltpu

F32 = jnp.float32
BF16 = jnp.bfloat16

RMS_EPS = 1e-5
HEAD_DIM = 64
N_DIFF_HEADS = 4
N_NA_HEADS = 8
GRID_W = 64
NA_KH = 8
NA_KW = 16
NA_GROUP = 8
NA_WIN = 16
TOP_K = 2
LANES = 128
NEG = -1e30
VMEM_LIMIT = 56 * 1024 * 1024

_NT = (((1,), (1,)), ((), ()))


def _rms(x, g):
    return x * lax.rsqrt(jnp.mean(x * x, axis=-1, keepdims=True) + RMS_EPS) * g


def _norm_proj_kernel(x_ref, g_ref, w_ref, o_ref, *, chunk):
    h = _rms(x_ref[...], g_ref[...]).astype(BF16)
    for c in range(o_ref.shape[1] // chunk):
        cols = slice(c * chunk, (c + 1) * chunk)
        o_ref[:, cols] = jnp.dot(h, w_ref[:, cols], preferred_element_type=F32).astype(BF16)


def norm_proj(x, g, w, tm=512):
    n, d = x.shape
    c = w.shape[1]
    return pl.pallas_call(
        functools.partial(_norm_proj_kernel, chunk=1024),
        grid=(n // tm,),
        in_specs=[pl.BlockSpec((tm, d), lambda i: (i, 0)),
                  pl.BlockSpec((1, d), lambda i: (0, 0)),
                  pl.BlockSpec((d, c), lambda i: (0, 0))],
        out_specs=pl.BlockSpec((tm, c), lambda i: (i, 0)),
        out_shape=jax.ShapeDtypeStruct((n, c), BF16),
        compiler_params=pltpu.CompilerParams(dimension_semantics=("parallel",),
                                             vmem_limit_bytes=VMEM_LIMIT),
        name="norm_proj")(x, g, w)


SCORE_CAP = 40.0
EXP_ZERO = 88.0


def _diff_attn_kernel(slopes_ref, q_ref, k_ref, v_ref, lam_ref, g_ref, o_ref,
                      k1_ref, k2_ref, va_ref, qv_ref, acc_ref, m_ref, kn_ref, *, t, lambda_init):
    h = pl.program_id(1)
    i = pl.program_id(2)
    n = k_ref.shape[1] // t
    slope = slopes_ref[h]
    lane = lax.broadcasted_iota(jnp.int32, (t, LANES), 1)
    low = lane < HEAD_DIM
    centred = (lax.broadcasted_iota(jnp.int32, (t, LANES), 0) - t // 2).astype(F32)

    def extras(base, a, b, c, d):
        return jnp.where(lane == base, a, jnp.where(lane == base + 1, b,
                         jnp.where(lane == base + 2, c, jnp.where(lane == base + 3, d, 0.0))))

    @pl.when(i == 0)
    def _prepare_keys():
        def body(j, norms):
            rows = pl.ds(pl.multiple_of(j * t, t), t)
            k = k_ref[0, rows, :].astype(F32)
            base = slope * jnp.asarray(j * t).astype(F32)
            k1_ref[rows, :] = jnp.where(low, k, extras(HEAD_DIM, -slope, slope * centred, 1.0, base)).astype(BF16)
            k2_ref[rows, :] = jnp.where(low, extras(0, -slope, slope * centred, 1.0, base), k).astype(BF16)
            va_ref[rows, :LANES] = v_ref[0, rows, :]
            va_ref[rows, LANES:] = jnp.ones((t, LANES), BF16)
            k2 = k * k
            n1 = jnp.sum(jnp.where(low, k2, 0.0), axis=1, keepdims=True)
            n2 = jnp.sum(jnp.where(low, 0.0, k2), axis=1, keepdims=True)
            return jnp.maximum(norms[0], n1), jnp.maximum(norms[1], n2)

        zero = jnp.zeros((t, 1), F32)
        n1, n2 = lax.fori_loop(0, n, body, (zero, zero))
        kn_ref[0] = jnp.max(n1)
        kn_ref[1] = jnp.max(n2)

    q = q_ref[0].astype(F32)
    qbase = -slope * jnp.asarray(i * t).astype(F32)
    for si, sign in enumerate((1.0, 0.0, -1.0)):
        qv_ref[0, si] = jnp.where(low, q, sign * extras(HEAD_DIM, centred, 1.0, qbase, 1.0)).astype(BF16)
        qv_ref[1, si] = jnp.where(low, sign * extras(0, centred, 1.0, qbase, 1.0), q).astype(BF16)
    before_q, diag_q, after_q = 0, 1, 2

    q2 = q * q
    u2 = jnp.maximum(jnp.sum(jnp.where(low, q2, 0.0), axis=1, keepdims=True) * kn_ref[0],
                     jnp.sum(jnp.where(low, 0.0, q2), axis=1, keepdims=True) * kn_ref[1])
    u = 1.01 * jnp.sqrt(u2)
    unshifted = jnp.max(u) <= SCORE_CAP
    reach = jnp.max(jnp.floor((EXP_ZERO + 2.0 * u) / (slope * t)) + 1.0).astype(jnp.int32)
    reach = jnp.where(unshifted, reach, n)
    j_lo = jnp.maximum(i - reach, 0)
    j_hi = jnp.minimum(i + reach, n - 1)

    acc_ref[...] = jnp.zeros(acc_ref.shape, F32)

    def scores(mi, qi, rows):
        km_ref = k1_ref if mi == 0 else k2_ref
        s = lax.dot_general(qv_ref[mi, qi], km_ref[rows, :], _NT, preferred_element_type=F32)
        if qi == diag_q:
            ri = lax.broadcasted_iota(jnp.int32, (t, t), 0)
            ci = lax.broadcasted_iota(jnp.int32, (t, t), 1)
            s = s - slope * jnp.abs(ri - ci).astype(F32)
        return s

    def plain_tile(j, qi, width=1):
        rows = pl.ds(pl.multiple_of(j * t, t), width * t)
        va = va_ref[rows, :]
        for mi in range(2):
            p = jnp.exp(scores(mi, qi, rows)).astype(BF16)
            acc_ref[mi] += jnp.dot(p, va, preferred_element_type=F32)

    def online_tile(j, qi, width=1):
        rows = pl.ds(pl.multiple_of(j * t, t), width * t)
        va = va_ref[rows, :]
        for mi in range(2):
            s = scores(mi, qi, rows)
            m_old = m_ref[mi]
            m_new = jnp.maximum(m_old, jnp.max(s, axis=1, keepdims=True))
            p = jnp.exp(s - m_new)
            acc_ref[mi] = (jnp.exp(m_old - m_new) * acc_ref[mi]
                           + jnp.dot(p.astype(BF16), va, preferred_element_type=F32))
            m_ref[mi] = m_new

    def walk(tile, width):
        def run(lo, hi, qi):
            odd = (hi - lo) % width
            for r in range(width - 1):
                @pl.when(r < odd)
                def _single():
                    tile(lo + r, qi)

            def body(c, carry):
                tile(lo + odd + c * width, qi, width)
                return carry
            lax.fori_loop(0, (hi - lo) // width, body, 0)

        run(j_lo, i, before_q)
        tile(i, diag_q)
        run(i + 1, j_hi + 1, after_q)

    @pl.when(unshifted)
    def _plain():
        walk(plain_tile, 2)

    @pl.when(jnp.logical_not(unshifted))
    def _online():
        m_ref[...] = jnp.full(m_ref.shape, NEG, F32)
        walk(online_tile, 1)

    a1 = acc_ref[0]
    a2 = acc_ref[1]
    lf = lam_ref[...]
    lam_full = (jnp.exp(jnp.sum(lf[0:1] * lf[1:2], axis=1, keepdims=True))
                - jnp.exp(jnp.sum(lf[2:3] * lf[3:4], axis=1, keepdims=True)) + lambda_init)
    o = a1[:, :LANES] / a1[:, LANES:] - lam_full * (a2[:, :LANES] / a2[:, LANES:])
    o_ref[0] = (_rms(o, g_ref[...]) * (1.0 - lambda_init)).astype(BF16)


def diff_attn(proj, lam, subln_g, layer_idx, t=512):
    b, s, _ = proj.shape
    nh = N_DIFF_HEADS
    assert (8 % nh) == 0 and t <= 512 and (t & (t - 1)) == 0
    slopes = jnp.asarray([2.0 ** (-8.0 * (h + 1) / nh) for h in range(nh)], F32)
    lambda_init = 0.8 - 0.6 * math.exp(-0.3 * layer_idx)
    grid_spec = pltpu.PrefetchScalarGridSpec(
        num_scalar_prefetch=1,
        grid=(b, nh, s // t),
        in_specs=[pl.BlockSpec((1, t, LANES), lambda bi, h, i, sl: (bi, i, h)),
                  pl.BlockSpec((1, s, LANES), lambda bi, h, i, sl: (bi, 0, nh + h)),
                  pl.BlockSpec((1, s, LANES), lambda bi, h, i, sl: (bi, 0, 2 * nh + h)),
                  pl.BlockSpec((4, HEAD_DIM), lambda bi, h, i, sl: (0, 0)),
                  pl.BlockSpec((1, 2 * HEAD_DIM), lambda bi, h, i, sl: (0, 0))],
        out_specs=pl.BlockSpec((1, t, LANES), lambda bi, h, i, sl: (bi, i, h)),
        scratch_shapes=[pltpu.VMEM((s, LANES), BF16),
                        pltpu.VMEM((s, LANES), BF16),
                        pltpu.VMEM((s, 2 * LANES), BF16),
                        pltpu.VMEM((2, 3, t, LANES), BF16),
                        pltpu.VMEM((2, t, 2 * LANES), F32),
                        pltpu.VMEM((2, t, 1), F32),
                        pltpu.SMEM((2,), F32)])
    return pl.pallas_call(
        functools.partial(_diff_attn_kernel, t=t, lambda_init=lambda_init),
        grid_spec=grid_spec,
        out_shape=jax.ShapeDtypeStruct((b, s, nh * 2 * HEAD_DIM), BF16),
        compiler_params=pltpu.CompilerParams(
            dimension_semantics=("arbitrary", "arbitrary", "arbitrary"),
            vmem_limit_bytes=VMEM_LIMIT),
        name="diff_attn")(slopes, proj, proj, proj, lam, subln_g.reshape(1, -1))


def _na_window_start(group, rows):
    return jnp.clip(group * NA_GROUP - NA_KH // 2, 0, rows - NA_WIN)


def _na_bias_tables(rpb, rows):
    n_rho, n_chi = 2 * NA_KH - 1, 2 * NA_KW - 1
    c = np.arange(GRID_W)[:, None]
    kc = np.arange(GRID_W)[None, :]
    cs = np.clip(c - NA_KW // 2, 0, GRID_W - NA_KW)
    col_ok = (kc >= cs) & (kc < cs + NA_KW)
    col_sel = ((kc - c + NA_KW - 1)[..., None] == np.arange(n_chi)) & col_ok[..., None]
    qr = np.arange(NA_GROUP)[:, None]
    kr = np.arange(NA_WIN)[None, :]
    row_sel, row_ok = [], []
    for r0 in (0, NA_GROUP, rows - NA_GROUP):
        ws = min(max(r0 - NA_KH // 2, 0), rows - NA_WIN)
        r = r0 + qr
        rs = np.clip(r - NA_KH // 2, 0, rows - NA_KH)
        ok = (ws + kr >= rs) & (ws + kr < rs + NA_KH)
        row_ok.append(ok)
        row_sel.append(((ws + kr - r + NA_KH - 1)[..., None] == np.arange(n_rho)) & ok[..., None])
    row_sel, row_ok = np.stack(row_sel), np.stack(row_ok)
    by_col = jnp.einsum("hrd,ckd->hrck", rpb.astype(F32), col_sel.astype(np.float32), precision="highest")
    bias = jnp.einsum("hrck,gqnr->ghqcnk", by_col, row_sel.astype(np.float32), precision="highest")
    valid = row_ok[:, None, :, None, :, None] & col_ok[None, None, None, :, None, :]
    shape = (3, rpb.shape[0], NA_GROUP * GRID_W, NA_WIN * GRID_W)
    return jnp.where(valid, bias, NEG).reshape(shape)


def _na_attn_kernel(q_ref, k_ref, v_ref, t_ref, o_ref, *, rows):
    g = pl.program_id(2)
    start = pl.multiple_of(_na_window_start(g, rows) * GRID_W, 4 * GRID_W)
    win = pl.ds(start, NA_WIN * GRID_W)
    kw = k_ref[0, win, :]
    vw = v_ref[0, win, :]
    low = lax.broadcasted_iota(jnp.int32, kw.shape, 1) < HEAD_DIM
    zero = jnp.zeros_like(kw)
    q = q_ref[0]
    out = None
    for hi, keep in enumerate((low, ~low)):
        s = lax.dot_general(q, jnp.where(keep, kw, zero), _NT, preferred_element_type=F32) + t_ref[0, hi]
        p = jnp.exp(s - jnp.max(s, axis=1, keepdims=True))
        l = jnp.sum(p, axis=1, keepdims=True)
        o = jnp.dot(p.astype(BF16), jnp.where(keep, vw, zero), preferred_element_type=F32) / l
        out = o if out is None else out + o
    o_ref[0] = out.astype(BF16)


def na_attn(proj, rpb, col0):
    b, s, _ = proj.shape
    rows = s // GRID_W
    assert rows % NA_GROUP == 0 and rows >= 2 * NA_WIN
    groups = rows // NA_GROUP
    pairs = N_NA_HEADS // 2
    cb = col0 // LANES
    tables = _na_bias_tables(rpb, rows)
    tq = NA_GROUP * GRID_W

    def case(g):
        return jnp.where(g == 0, 0, jnp.where(g == groups - 1, 2, 1))

    return pl.pallas_call(
        functools.partial(_na_attn_kernel, rows=rows),
        grid=(pairs, b, groups),
        in_specs=[pl.BlockSpec((1, tq, LANES), lambda p, bi, g: (bi, g, cb + p)),
                  pl.BlockSpec((1, s, LANES), lambda p, bi, g: (bi, 0, cb + pairs + p)),
                  pl.BlockSpec((1, s, LANES), lambda p, bi, g: (bi, 0, cb + 2 * pairs + p)),
                  pl.BlockSpec((1, 2, tq, NA_WIN * GRID_W), lambda p, bi, g: (case(g), p, 0, 0))],
        out_specs=pl.BlockSpec((1, tq, LANES), lambda p, bi, g: (bi, g, p)),
        out_shape=jax.ShapeDtypeStruct((b, s, N_NA_HEADS * HEAD_DIM), BF16),
        compiler_params=pltpu.CompilerParams(
            dimension_semantics=("parallel", "parallel", "parallel"),
            vmem_limit_bytes=VMEM_LIMIT),
        name="na_attn")(proj, proj, proj, tables)


def _out_proj_kernel(x_ref, a_ref, b_ref, wa_ref, wb_ref, o_ref):
    o_ref[...] = (x_ref[...]
                  + jnp.dot(a_ref[...], wa_ref[...], preferred_element_type=F32)
                  + jnp.dot(b_ref[...], wb_ref[...], preferred_element_type=F32))


def out_proj(x, oa, ob, wa, wb, tm=512):
    n, d = x.shape
    return pl.pallas_call(
        _out_proj_kernel,
        grid=(n // tm,),
        in_specs=[pl.BlockSpec((tm, d), lambda i: (i, 0)),
                  pl.BlockSpec((tm, oa.shape[1]), lambda i: (i, 0)),
                  pl.BlockSpec((tm, ob.shape[1]), lambda i: (i, 0)),
                  pl.BlockSpec(wa.shape, lambda i: (0, 0)),
                  pl.BlockSpec(wb.shape, lambda i: (0, 0))],
        out_specs=pl.BlockSpec((tm, d), lambda i: (i, 0)),
        out_shape=jax.ShapeDtypeStruct((n, d), F32),
        compiler_params=pltpu.CompilerParams(dimension_semantics=("parallel",),
                                             vmem_limit_bytes=VMEM_LIMIT),
        name="out_proj")(x, oa, ob, wa, wb)


def _swiglu_act(a, b):
    return (a * jax.nn.sigmoid(a) * b).astype(BF16)


def _ffn_dense_kernel(x_ref, g_ref, w1_ref, w3_ref, w2_ref, gf_ref, o_ref, *, chunk, final_norm):
    x = x_ref[...]
    h = _rms(x, g_ref[...]).astype(BF16)
    y = x
    for c in range(w1_ref.shape[1] // chunk):
        cols = slice(c * chunk, (c + 1) * chunk)
        act = _swiglu_act(jnp.dot(h, w1_ref[:, cols], preferred_element_type=F32),
                          jnp.dot(h, w3_ref[:, cols], preferred_element_type=F32))
        y = y + jnp.dot(act, w2_ref[cols, :], preferred_element_type=F32)
    o_ref[...] = _rms(y, gf_ref[...]) if final_norm else y


def ffn_dense(x, g, w1, w3, w2, g_final, final_norm, tm=512):
    n, d = x.shape
    f = w1.shape[1]
    chunk = f // 2 if (f // 2) % LANES == 0 else f
    resident = dict(pipeline_mode=pl.Buffered(1))
    return pl.pallas_call(
        functools.partial(_ffn_dense_kernel, chunk=chunk, final_norm=final_norm),
        grid=(n // tm,),
        in_specs=[pl.BlockSpec((tm, d), lambda i: (i, 0)),
                  pl.BlockSpec((1, d), lambda i: (0, 0)),
                  pl.BlockSpec((d, f), lambda i: (0, 0), **resident),
                  pl.BlockSpec((d, f), lambda i: (0, 0), **resident),
                  pl.BlockSpec((f, d), lambda i: (0, 0), **resident),
                  pl.BlockSpec((1, d), lambda i: (0, 0))],
        out_specs=pl.BlockSpec((tm, d), lambda i: (i, 0)),
        out_shape=jax.ShapeDtypeStruct((n, d), F32),
        compiler_params=pltpu.CompilerParams(dimension_semantics=("parallel",),
                                             vmem_limit_bytes=VMEM_LIMIT),
        name="ffn_dense")(x, g, w1, w3, w2, g_final)


MOE_TT = 512
MOE_CH = 16
MOE_R = 2 * MOE_TT + LANES
MOE_LCH = MOE_R // MOE_CH


def _moe_route_kernel(x_ref, g_ref, rt_ref, stage_ref, cnt_ref, info_ref, *, n_tiles):
    tt = x_ref.shape[0]
    ne = rt_ref.shape[0]

    @pl.when(pl.program_id(0) == n_tiles)
    def _zero_tile():
        stage_ref[...] = jnp.zeros(stage_ref.shape, BF16)
        cnt_ref[...] = jnp.zeros(cnt_ref.shape, F32)
        info_ref[...] = jnp.zeros(info_ref.shape, F32)

    @pl.when(pl.program_id(0) < n_tiles)
    def _route():
        h = _rms(x_ref[...], g_ref[...]).astype(BF16)
        logits = lax.dot_general(rt_ref[...], h, _NT, preferred_element_type=F32)
        erow = lax.broadcasted_iota(jnp.int32, (ne, tt), 0)
        v1 = jnp.max(logits, axis=0, keepdims=True)
        i1 = jnp.min(jnp.where(logits == v1, erow, ne), axis=0, keepdims=True)
        rest = jnp.where(erow == i1, -jnp.inf, logits)
        v2 = jnp.max(rest, axis=0, keepdims=True)
        i2 = jnp.min(jnp.where(rest == v2, erow, ne), axis=0, keepdims=True)
        e2 = jnp.exp(v2 - v1)
        g1 = 1.0 / (1.0 + e2)
        g2 = e2 / (1.0 + e2)
        member = jnp.where((erow == i1) | (erow == i2), 1.0, 0.0)
        before = (lax.broadcasted_iota(jnp.int32, (tt, tt), 0)
                  < lax.broadcasted_iota(jnp.int32, (tt, tt), 1))
        cum = jnp.dot(member.astype(BF16), jnp.where(before, 1.0, 0.0).astype(BF16),
                      preferred_element_type=F32)
        cnt = jnp.sum(member, axis=1, keepdims=True)
        padded = jnp.floor((cnt + (MOE_CH - 1)) * (1.0 / MOE_CH)) * MOE_CH
        padded_b = jnp.broadcast_to(padded, (ne, tt))
        offset = jnp.zeros((1, tt), F32)
        rank1 = jnp.zeros((1, tt), F32)
        rank2 = jnp.zeros((1, tt), F32)
        for e in range(ne):
            pos = offset + cum[e:e + 1]
            rank1 = rank1 + jnp.where(i1 == e, pos, 0.0)
            rank2 = rank2 + jnp.where(i2 == e, pos, 0.0)
            offset = offset + padded_b[e:e + 1]
        r = lax.broadcasted_iota(jnp.int32, (MOE_R, tt), 0).astype(F32)
        perm = jnp.where((r == rank1) | (r == rank2), 1.0, 0.0).astype(BF16)
        stage_ref[0] = jnp.dot(perm, h, preferred_element_type=F32).astype(BF16)
        cnt_ref[0] = jnp.broadcast_to(padded, (ne, LANES))
        irow = lax.broadcasted_iota(jnp.int32, (8, tt), 0)
        info_ref[0] = jnp.where(irow == 0, rank1, jnp.where(irow == 1, rank2,
                                jnp.where(irow == 2, g1, jnp.where(irow == 3, g2, 0.0))))


def moe_route(x, g, router_t):
    n, d = x.shape
    ne = router_t.shape[0]
    assert ne == 8 and n % MOE_TT == 0
    nt = n // MOE_TT
    return pl.pallas_call(
        functools.partial(_moe_route_kernel, n_tiles=nt),
        grid=(nt + 1,),
        in_specs=[pl.BlockSpec((MOE_TT, d), lambda i: (jnp.minimum(i, nt - 1), 0)),
                  pl.BlockSpec((1, d), lambda i: (0, 0)),
                  pl.BlockSpec((ne, d), lambda i: (0, 0))],
        out_specs=[pl.BlockSpec((1, MOE_R, d), lambda i: (i, 0, 0)),
                   pl.BlockSpec((1, ne, LANES), lambda i: (i, 0, 0)),
                   pl.BlockSpec((1, 8, MOE_TT), lambda i: (i, 0, 0))],
        out_shape=[jax.ShapeDtypeStruct((nt + 1, MOE_R, d), BF16),
                   jax.ShapeDtypeStruct((nt + 1, ne, LANES), F32),
                   jax.ShapeDtypeStruct((nt + 1, 8, MOE_TT), F32)],
        compiler_params=pltpu.CompilerParams(dimension_semantics=("arbitrary",),
                                             vmem_limit_bytes=VMEM_LIMIT),
        name="moe_route")(x, g, router_t)


def _moe_tables(padded_counts, tm):
    nt, ne = padded_counts.shape
    tile_ch = tm // MOE_CH
    lc = padded_counts // MOE_CH
    loc = jnp.cumsum(lc, axis=1) - lc
    per_expert = jnp.sum(lc, axis=0)
    per_expert_pad = (per_expert + tile_ch - 1) // tile_ch * tile_ch
    expert_end = jnp.cumsum(per_expert_pad)
    expert_base = expert_end - per_expert_pad
    seg_dst = expert_base[None, :] + jnp.cumsum(lc, axis=0) - lc
    max_chunks = (2 * nt * MOE_TT) // MOE_CH + nt * ne + ne * (tile_ch - 1)
    n_tiles = -(-max_chunks // tile_ch)
    n_active = (expert_end[-1] // tile_ch).astype(jnp.int32)
    tile_expert = jnp.minimum(jnp.searchsorted(expert_end // tile_ch, jnp.arange(n_tiles), side="right"),
                              ne - 1).astype(jnp.int32)
    start = seg_dst.T.reshape(-1)
    length = lc.T.reshape(-1)
    source = (jnp.arange(nt)[None, :] * MOE_LCH + loc.T).reshape(-1)
    d = jnp.arange(n_tiles * tile_ch)
    seg = jnp.clip(jnp.searchsorted(start, d, side="right") - 1, 0, nt * ne - 1)
    inside = (d - start[seg]) < length[seg]
    zero_chunk = nt * MOE_LCH
    ffn_src = jnp.where(inside, source[seg] + d - start[seg], zero_chunk).astype(jnp.int32)
    l = jnp.arange(MOE_LCH)
    seg_end = loc + lc
    e_of = jnp.sum(l[None, :, None] >= seg_end[:, None, :], axis=2)
    e_c = jnp.minimum(e_of, ne - 1)
    cmb = jnp.take_along_axis(seg_dst, e_c, axis=1) + l[None, :] - jnp.take_along_axis(loc, e_c, axis=1)
    n_valid = jnp.sum(lc, axis=1).astype(jnp.int32)
    cmb = jnp.where(l[None, :] < n_valid[:, None], cmb, 0).astype(jnp.int32)
    return tile_expert, n_active.reshape(1), ffn_src, cmb.reshape(-1), n_valid, n_tiles


def _chunk_copies(table_ref, first, count, src_hbm, dst_ref, sem, start):
    def body(c, carry):
        row = pl.multiple_of(table_ref[first + c] * MOE_CH, MOE_CH)
        copy = pltpu.make_async_copy(src_hbm.at[pl.ds(row, MOE_CH), :],
                                     dst_ref.at[pl.ds(pl.multiple_of(c * MOE_CH, MOE_CH), MOE_CH), :], sem)
        if start:
            copy.start()
        else:
            copy.wait()
        return carry
    lax.fori_loop(0, count, body, 0)


def _moe_ffn_kernel(te_ref, nact_ref, src_ref, stage_hbm, w1_ref, w3_ref, w2_ref, o_ref,
                    xbuf, sem, acc_ref, *, tm):
    i = pl.program_id(0)
    j = pl.program_id(1)
    nf = pl.num_programs(1)
    nch = tm // MOE_CH
    nact = nact_ref[0]
    slot = i % 2

    @pl.when((i == 0) & (j == 0))
    def _prime():
        _chunk_copies(src_ref, 0, nch, stage_hbm, xbuf.at[0], sem.at[0], True)

    @pl.when((i < nact) & (j == 0))
    def _rotate():
        _chunk_copies(src_ref, i * nch, nch, stage_hbm, xbuf.at[slot], sem.at[slot], False)

        @pl.when(i + 1 < nact)
        def _prefetch():
            _chunk_copies(src_ref, (i + 1) * nch, nch, stage_hbm, xbuf.at[1 - slot], sem.at[1 - slot], True)

    @pl.when(i < nact)
    def _compute():
        x = xbuf[slot]
        act = _swiglu_act(jnp.dot(x, w1_ref[0], preferred_element_type=F32),
                          jnp.dot(x, w3_ref[0], preferred_element_type=F32))
        part = jnp.dot(act, w2_ref[0], preferred_element_type=F32)

        @pl.when(j == 0)
        def _first():
            acc_ref[...] = part

        @pl.when(j > 0)
        def _rest():
            acc_ref[...] += part

        @pl.when(j == nf - 1)
        def _store():
            o_ref[...] = acc_ref[...].astype(BF16)

    @pl.when((i >= nact) & (j == nf - 1))
    def _unused_tile():
        o_ref[...] = jnp.zeros(o_ref.shape, BF16)


def moe_ffn(stage, tile_expert, n_active, ffn_src, w1, w3, w2, n_tiles, tm, tf=512):
    d = stage.shape[-1]
    f = w1.shape[2]
    tf = tf if f % tf == 0 else f
    nf = f // tf

    def live(i, na):
        return jnp.minimum(i, na[0] - 1)

    def fcol(i, j, na):
        return jnp.where(i < na[0], j, nf - 1)

    grid_spec = pltpu.PrefetchScalarGridSpec(
        num_scalar_prefetch=3,
        grid=(n_tiles, nf),
        in_specs=[pl.BlockSpec(memory_space=pl.ANY),
                  pl.BlockSpec((1, d, tf), lambda i, j, te, na, src: (te[live(i, na)], 0, fcol(i, j, na))),
                  pl.BlockSpec((1, d, tf), lambda i, j, te, na, src: (te[live(i, na)], 0, fcol(i, j, na))),
                  pl.BlockSpec((1, tf, d), lambda i, j, te, na, src: (te[live(i, na)], fcol(i, j, na), 0))],
        out_specs=pl.BlockSpec((tm, d), lambda i, j, te, na, src: (i, 0)),
        scratch_shapes=[pltpu.VMEM((2, tm, d), BF16),
                        pltpu.SemaphoreType.DMA((2,)),
                        pltpu.VMEM((tm, d), F32)])
    return pl.pallas_call(
        functools.partial(_moe_ffn_kernel, tm=tm),
        grid_spec=grid_spec,
        out_shape=jax.ShapeDtypeStruct((n_tiles * tm, d), BF16),
        compiler_params=pltpu.CompilerParams(dimension_semantics=("arbitrary", "arbitrary"),
                                             vmem_limit_bytes=VMEM_LIMIT),
        name="moe_ffn")(tile_expert, n_active, ffn_src, stage.reshape(-1, d), w1, w3, w2)


def _moe_combine_kernel(cmb_ref, nval_ref, x_ref, info_ref, ys_hbm, gf_ref, o_ref, ybuf, sem, *, final_norm):
    t = pl.program_id(0)
    nt = pl.num_programs(0)
    slot = t % 2

    def fetch(tile, slot, start):
        nv = nval_ref[tile]
        _chunk_copies(cmb_ref, tile * MOE_LCH, nv, ys_hbm, ybuf.at[slot], sem.at[slot], start)
        if start:
            def clear(l, carry):
                ybuf[slot, pl.ds(pl.multiple_of(l * MOE_CH, MOE_CH), MOE_CH), :] = jnp.zeros(
                    (MOE_CH, ybuf.shape[2]), BF16)
                return carry
            lax.fori_loop(nv, MOE_LCH, clear, 0)

    @pl.when(t == 0)
    def _prime():
        fetch(0, 0, True)

    fetch(t, slot, False)

    @pl.when(t + 1 < nt)
    def _prefetch():
        fetch(t + 1, 1 - slot, True)

    info = info_ref[...]
    tt = info.shape[0]
    col = lax.broadcasted_iota(jnp.int32, (tt, MOE_R), 1).astype(F32)
    pick = jnp.concatenate([jnp.where(col == info[:, 0:1], 1.0, 0.0).astype(BF16),
                            jnp.where(col == info[:, 1:2], 1.0, 0.0).astype(BF16)], axis=0)
    y = jnp.dot(pick, ybuf[slot], preferred_element_type=F32)
    out = x_ref[...] + info[:, 2:3] * y[:tt] + info[:, 3:4] * y[tt:]
    o_ref[...] = _rms(out, gf_ref[...]) if final_norm else out


def moe_combine(x, info_cols, ys, cmb, n_valid, g_final, final_norm):
    n, d = x.shape
    nt = n // MOE_TT
    grid_spec = pltpu.PrefetchScalarGridSpec(
        num_scalar_prefetch=2,
        grid=(nt,),
        in_specs=[pl.BlockSpec((MOE_TT, d), lambda t, c, v: (t, 0)),
                  pl.BlockSpec((MOE_TT, 8), lambda t, c, v: (t, 0)),
                  pl.BlockSpec(memory_space=pl.ANY),
                  pl.BlockSpec((1, d), lambda t, c, v: (0, 0))],
        out_specs=pl.BlockSpec((MOE_TT, d), lambda t, c, v: (t, 0)),
        scratch_shapes=[pltpu.VMEM((2, MOE_R, d), BF16),
                        pltpu.SemaphoreType.DMA((2,))])
    return pl.pallas_call(
        functools.partial(_moe_combine_kernel, final_norm=final_norm),
        grid_spec=grid_spec,
        out_shape=jax.ShapeDtypeStruct((n, d), F32),
        compiler_params=pltpu.CompilerParams(dimension_semantics=("arbitrary",),
                                             vmem_limit_bytes=VMEM_LIMIT),
        name="moe_combine")(cmb, n_valid, x, info_cols, ys, g_final)


def moe_top2(x, g, router, w1, w3, w2, g_final, final_norm, tm=1024):
    n, d = x.shape
    nt = n // MOE_TT
    stage, counts, info = moe_route(x, g, router.T.astype(BF16))
    padded_counts = counts[:nt, :, 0].astype(jnp.int32)
    tile_expert, n_active, ffn_src, cmb, n_valid, n_tiles = _moe_tables(padded_counts, tm)
    ys = moe_ffn(stage, tile_expert, n_active, ffn_src, w1, w3, w2, n_tiles, tm)
    info_cols = info[:nt].transpose(0, 2, 1).reshape(n, 8)
    return moe_combine(x, info_cols, ys, cmb, n_valid, g_final, final_norm)


def _trunk(x, w_in, w_out, norm_mix, norm_ffn, lambda_qk, subln_g, na_rpb,
           ffn_w1, ffn_w3, ffn_w2, moe_router, moe_w1, moe_w3, moe_w2, norm_final):
    b, s, d = x.shape
    depth = w_in.shape[0]
    diff_w = N_DIFF_HEADS * 2 * HEAD_DIM
    na_w = N_NA_HEADS * HEAD_DIM
    scale = HEAD_DIM ** -0.5
    col_scale = jnp.ones((w_in.shape[2],), F32)
    col_scale = col_scale.at[:diff_w].set(scale).at[3 * diff_w:3 * diff_w + na_w].set(scale)
    x = x.reshape(b * s, d)
    for l in range(depth):
        proj = norm_proj(x, norm_mix[l].reshape(1, d), (w_in[l] * col_scale).astype(BF16))
        proj = proj.reshape(b, s, -1)
        oa = diff_attn(proj, lambda_qk[l], subln_g[l], l).reshape(b * s, diff_w)
        ob = na_attn(proj, na_rpb[l], 3 * diff_w).reshape(b * s, na_w)
        wo = w_out[l].astype(BF16)
        x = out_proj(x, oa, ob, wo[:diff_w], wo[diff_w:])
        last = l == depth - 1
        i = l // 2
        if l % 2 == 0:
            x = ffn_dense(x, norm_ffn[l].reshape(1, d), ffn_w1[i].astype(BF16),
                          ffn_w3[i].astype(BF16), ffn_w2[i].astype(BF16),
                          norm_final.reshape(1, d), final_norm=last)
        else:
            x = moe_top2(x, norm_ffn[l].reshape(1, d), moe_router[i], moe_w1[i].astype(BF16),
                         moe_w3[i].astype(BF16), moe_w2[i].astype(BF16),
                         norm_final.reshape(1, d), final_norm=last)
    return x.reshape(b, s, d)


def kernel(x_prompt, x_sample, w_in, w_out, norm_mix, norm_ffn, lambda_qk, subln_g, na_rpb,
           ffn_w1, ffn_w3, ffn_w2, moe_router, moe_w1, moe_w3, moe_w2, norm_final):
    assert x_prompt.shape[1:] == x_sample.shape[1:]
    nb = x_prompt.shape[0]
    y = _trunk(jnp.concatenate([x_prompt, x_sample], axis=0),
               w_in, w_out, norm_mix, norm_ffn, lambda_qk, subln_g, na_rpb,
               ffn_w1, ffn_w3, ffn_w2, moe_router, moe_w1, moe_w3, moe_w2, norm_final)
    return (y[:nb], y[nb:])
```

```python
import functools
import math

import numpy as np
import jax
import jax.numpy as jnp
from jax import lax
from jax.experimental import pallas as pl
from jax.experimental.pallas import tpu as pltpu

F32 = jnp.float32
BF16 = jnp.bfloat16

RMS_EPS = 1e-5
HEAD_DIM = 64
N_DIFF_HEADS = 4
N_NA_HEADS = 8
GRID_W = 64
NA_KH = 8
NA_KW = 16
NA_GROUP = 8
NA_WIN = 16
TOP_K = 2
LANES = 128
NEG = -1e30
VMEM_LIMIT = 56 * 1024 * 1024

_NT = (((1,), (1,)), ((), ()))


def _rms(x, g):
    return x * lax.rsqrt(jnp.mean(x * x, axis=-1, keepdims=True) + RMS_EPS) * g


def _row_tile_specs(parts, tm):
    first = parts[0].shape[0] // tm
    d = parts[0].shape[1]
    specs = [pl.BlockSpec((tm, d), lambda i: (jnp.minimum(i, first - 1), 0))]
    if len(parts) == 2:
        assert parts[0].shape[0] % tm == 0 and parts[1].shape[0] % tm == 0
        specs.append(pl.BlockSpec((tm, d), lambda i: (jnp.maximum(i - first, 0), 0)))
    return specs, first


def _row_tile(x_refs, first):
    if len(x_refs) == 1:
        return x_refs[0][...]
    return jnp.where(pl.program_id(0) < first, x_refs[0][...], x_refs[1][...])


def _norm_proj_kernel(*refs, n_x, first, chunk):
    g_ref, w_ref, o_ref = refs[n_x:]
    h = _rms(_row_tile(refs[:n_x], first), g_ref[...]).astype(BF16)
    for c in range(o_ref.shape[1] // chunk):
        cols = slice(c * chunk, (c + 1) * chunk)
        o_ref[:, cols] = jnp.dot(h, w_ref[:, cols], preferred_element_type=F32).astype(BF16)


def norm_proj(x_parts, g, w, tm=512):
    n = sum(p.shape[0] for p in x_parts)
    d, c = w.shape
    x_specs, first = _row_tile_specs(x_parts, tm)
    return pl.pallas_call(
        functools.partial(_norm_proj_kernel, n_x=len(x_parts), first=first, chunk=1024),
        grid=(n // tm,),
        in_specs=x_specs + [pl.BlockSpec((1, d), lambda i: (0, 0)),
                            pl.BlockSpec((d, c), lambda i: (0, 0))],
        out_specs=pl.BlockSpec((tm, c), lambda i: (i, 0)),
        out_shape=jax.ShapeDtypeStruct((n, c), BF16),
        compiler_params=pltpu.CompilerParams(dimension_semantics=("parallel",),
                                             vmem_limit_bytes=VMEM_LIMIT),
        name="norm_proj")(*x_parts, g, w)


SCORE_CAP = 40.0
EXP_ZERO = 88.0


def _diff_attn_kernel(slopes_ref, q_ref, k_ref, v_ref, lam_ref, g_ref, o_ref,
                      k1_ref, k2_ref, va_ref, qv_ref, dist_ref, acc_ref, m_ref, kn_ref,
                      *, t, whole, lambda_init):
    h = pl.program_id(1)
    i = pl.program_id(2)
    n = k_ref.shape[1] // t
    slope = slopes_ref[h]
    lane = lax.broadcasted_iota(jnp.int32, (t, LANES), 1)
    low = lane < HEAD_DIM
    centred = (lax.broadcasted_iota(jnp.int32, (t, LANES), 0) - t // 2).astype(F32)

    def extras(base, a, b, c, d):
        return jnp.where(lane == base, a, jnp.where(lane == base + 1, b,
                         jnp.where(lane == base + 2, c, jnp.where(lane == base + 3, d, 0.0))))

    @pl.when(i == 0)
    def _prepare_keys():
        def body(j, norms):
            rows = pl.ds(pl.multiple_of(j * t, t), t)
            k = k_ref[0, rows, :].astype(F32)
            base = slope * jnp.asarray(j * t).astype(F32)
            k1_ref[rows, :] = jnp.where(low, k, extras(HEAD_DIM, -slope, slope * centred, 1.0, base)).astype(BF16)
            k2_ref[rows, :] = jnp.where(low, extras(0, -slope, slope * centred, 1.0, base), k).astype(BF16)
            va_ref[rows, :LANES] = v_ref[0, rows, :]
            va_ref[rows, LANES:] = jnp.ones((t, LANES), BF16)
            k2 = k * k
            n1 = jnp.sum(jnp.where(low, k2, 0.0), axis=1, keepdims=True)
            n2 = jnp.sum(jnp.where(low, 0.0, k2), axis=1, keepdims=True)
            return jnp.maximum(norms[0], n1), jnp.maximum(norms[1], n2)

        zero = jnp.zeros((t, 1), F32)
        n1, n2 = lax.fori_loop(0, n, body, (zero, zero))
        kn_ref[0] = jnp.max(n1)
        kn_ref[1] = jnp.max(n2)
        dist_ref[...] = slope * jnp.abs(lax.broadcasted_iota(jnp.int32, (t, t), 0)
                                        - lax.broadcasted_iota(jnp.int32, (t, t), 1)).astype(F32)

    q = q_ref[0].astype(F32)
    qbase = -slope * jnp.asarray(i * t).astype(F32)
    for si, sign in enumerate((1.0, 0.0, -1.0)):
        qv_ref[0, si] = jnp.where(low, q, sign * extras(HEAD_DIM, centred, 1.0, qbase, 1.0)).astype(BF16)
        qv_ref[1, si] = jnp.where(low, sign * extras(0, centred, 1.0, qbase, 1.0), q).astype(BF16)
    before_q, diag_q, after_q = 0, 1, 2

    q2 = q * q
    u2 = jnp.maximum(jnp.sum(jnp.where(low, q2, 0.0), axis=1, keepdims=True) * kn_ref[0],
                     jnp.sum(jnp.where(low, 0.0, q2), axis=1, keepdims=True) * kn_ref[1])
    u = 1.01 * jnp.sqrt(u2)
    unshifted = jnp.max(u) <= SCORE_CAP
    reach = jnp.max(jnp.floor((EXP_ZERO + 2.0 * u) / (slope * t)) + 1.0).astype(jnp.int32)
    reach = jnp.where(unshifted, reach, n)
    j_lo = jnp.maximum(i - reach, 0)
    j_hi = jnp.minimum(i + reach, n - 1)

    acc_ref[...] = jnp.zeros(acc_ref.shape, F32)

    def scores(mi, qi, rows):
        km_ref = k1_ref if mi == 0 else k2_ref
        s = lax.dot_general(qv_ref[mi, qi], km_ref[rows, :], _NT, preferred_element_type=F32)
        if qi == diag_q:
            s = s - dist_ref[...]
        return s

    def plain_tile(j, qi, width=1):
        rows = pl.ds(pl.multiple_of(j * t, t), width * t)
        va = va_ref[rows, :]
        for mi in range(2):
            p = jnp.exp(scores(mi, qi, rows)).astype(BF16)
            acc_ref[mi] += jnp.dot(p, va, preferred_element_type=F32)

    def online_tile(j, qi, width=1):
        rows = pl.ds(pl.multiple_of(j * t, t), width * t)
        va = va_ref[rows, :]
        for mi in range(2):
            s = scores(mi, qi, rows)
            m_old = m_ref[mi]
            m_new = jnp.maximum(m_old, jnp.max(s, axis=1, keepdims=True))
            p = jnp.exp(s - m_new)
            acc_ref[mi] = (jnp.exp(m_old - m_new) * acc_ref[mi]
                           + jnp.dot(p.astype(BF16), va, preferred_element_type=F32))
            m_ref[mi] = m_new

    def walk(tile, width):
        def run(lo, hi, qi):
            odd = (hi - lo) % width
            for r in range(width - 1):
                @pl.when(r < odd)
                def _single():
                    tile(lo + r, qi)

            def body(c, carry):
                tile(lo + odd + c * width, qi, width)
                return carry
            lax.fori_loop(0, (hi - lo) // width, body, 0)

        run(j_lo, i, before_q)
        tile(i, diag_q)
        run(i + 1, j_hi + 1, after_q)

    def whole_row_step(c, carry):
        for mi in range(2):
            km_ref = k1_ref if mi == 0 else k2_ref
            parts = []
            for w in range(whole):
                j = c * whole + w
                rows = pl.ds(pl.multiple_of(j * t, t), t)
                s = lax.dot_general(qv_ref[mi, jnp.clip(j - i, -1, 1) + 1], km_ref[rows, :], _NT,
                                    preferred_element_type=F32)
                s = s - jnp.where(j == i, 1.0, 0.0) * dist_ref[...]
                parts.append(jnp.dot(jnp.exp(s).astype(BF16), va_ref[rows, :], preferred_element_type=F32))
            acc_ref[mi] += sum(parts)
        return carry

    every_tile = (j_lo == 0) & (j_hi == n - 1)

    @pl.when(unshifted & every_tile)
    def _plain_whole_row():
        lax.fori_loop(0, n // whole, whole_row_step, 0)

    @pl.when(unshifted & jnp.logical_not(every_tile))
    def _plain():
        walk(plain_tile, 2)

    @pl.when(jnp.logical_not(unshifted))
    def _online():
        m_ref[...] = jnp.full(m_ref.shape, NEG, F32)
        walk(online_tile, 1)

    a1 = acc_ref[0]
    a2 = acc_ref[1]
    lf = lam_ref[...]
    lam_full = (jnp.exp(jnp.sum(lf[0:1] * lf[1:2], axis=1, keepdims=True))
                - jnp.exp(jnp.sum(lf[2:3] * lf[3:4], axis=1, keepdims=True)) + lambda_init)
    o = a1[:, :LANES] / a1[:, LANES:] - lam_full * (a2[:, :LANES] / a2[:, LANES:])
    o_ref[0] = (_rms(o, g_ref[...]) * (1.0 - lambda_init)).astype(BF16)


def diff_attn(proj, lam, subln_g, layer_idx, t=512):
    b, s, _ = proj.shape
    nh = N_DIFF_HEADS
    assert (8 % nh) == 0 and t <= 512 and (t & (t - 1)) == 0
    slopes = jnp.asarray([2.0 ** (-8.0 * (h + 1) / nh) for h in range(nh)], F32)
    lambda_init = 0.8 - 0.6 * math.exp(-0.3 * layer_idx)
    grid_spec = pltpu.PrefetchScalarGridSpec(
        num_scalar_prefetch=1,
        grid=(b, nh, s // t),
        in_specs=[pl.BlockSpec((1, t, LANES), lambda bi, h, i, sl: (bi, i, h)),
                  pl.BlockSpec((1, s, LANES), lambda bi, h, i, sl: (bi, 0, nh + h)),
                  pl.BlockSpec((1, s, LANES), lambda bi, h, i, sl: (bi, 0, 2 * nh + h)),
                  pl.BlockSpec((4, HEAD_DIM), lambda bi, h, i, sl: (0, 0)),
                  pl.BlockSpec((1, 2 * HEAD_DIM), lambda bi, h, i, sl: (0, 0))],
        out_specs=pl.BlockSpec((1, t, LANES), lambda bi, h, i, sl: (bi, i, h)),
        scratch_shapes=[pltpu.VMEM((s, LANES), BF16),
                        pltpu.VMEM((s, LANES), BF16),
                        pltpu.VMEM((s, 2 * LANES), BF16),
                        pltpu.VMEM((2, 3, t, LANES), BF16),
                        pltpu.VMEM((t, t), F32),
                        pltpu.VMEM((2, t, 2 * LANES), F32),
                        pltpu.VMEM((2, t, 1), F32),
                        pltpu.SMEM((2,), F32)])
    whole = math.gcd(s // t, 16)
    return pl.pallas_call(
        functools.partial(_diff_attn_kernel, t=t, whole=whole, lambda_init=lambda_init),
        grid_spec=grid_spec,
        out_shape=jax.ShapeDtypeStruct((b, s, nh * 2 * HEAD_DIM), BF16),
        compiler_params=pltpu.CompilerParams(
            dimension_semantics=("arbitrary", "arbitrary", "arbitrary"),
            vmem_limit_bytes=VMEM_LIMIT),
        name="diff_attn")(slopes, proj, proj, proj, lam, subln_g.reshape(1, -1))


def _na_window_start(group, rows):
    return jnp.clip(group * NA_GROUP - NA_KH // 2, 0, rows - NA_WIN)


def _na_bias_tables(rpb, rows):
    n_rho, n_chi = 2 * NA_KH - 1, 2 * NA_KW - 1
    c = np.arange(GRID_W)[:, None]
    kc = np.arange(GRID_W)[None, :]
    cs = np.clip(c - NA_KW // 2, 0, GRID_W - NA_KW)
    col_ok = (kc >= cs) & (kc < cs + NA_KW)
    col_sel = ((kc - c + NA_KW - 1)[..., None] == np.arange(n_chi)) & col_ok[..., None]
    qr = np.arange(NA_GROUP)[:, None]
    kr = np.arange(NA_WIN)[None, :]
    row_sel, row_ok = [], []
    for r0 in (0, NA_GROUP, rows - NA_GROUP):
        ws = min(max(r0 - NA_KH // 2, 0), rows - NA_WIN)
        r = r0 + qr
        rs = np.clip(r - NA_KH // 2, 0, rows - NA_KH)
        ok = (ws + kr >= rs) & (ws + kr < rs + NA_KH)
        row_ok.append(ok)
        row_sel.append(((ws + kr - r + NA_KH - 1)[..., None] == np.arange(n_rho)) & ok[..., None])
    row_sel, row_ok = np.stack(row_sel), np.stack(row_ok)
    by_col = jnp.einsum("hrd,ckd->hrck", rpb.astype(F32), col_sel.astype(np.float32), precision="highest")
    bias = jnp.einsum("hrck,gqnr->ghqcnk", by_col, row_sel.astype(np.float32), precision="highest")
    valid = row_ok[:, None, :, None, :, None] & col_ok[None, None, None, :, None, :]
    shape = (3, rpb.shape[0], NA_GROUP * GRID_W, NA_WIN * GRID_W)
    return jnp.where(valid, bias, NEG).reshape(shape)


def _na_attn_kernel(q_ref, k_ref, v_ref, t_ref, o_ref, *, rows):
    g = pl.program_id(2)
    start = pl.multiple_of(_na_window_start(g, rows) * GRID_W, 4 * GRID_W)
    win = pl.ds(start, NA_WIN * GRID_W)
    kw = k_ref[0, win, :]
    vw = v_ref[0, win, :]
    low = lax.broadcasted_iota(jnp.int32, kw.shape, 1) < HEAD_DIM
    zero = jnp.zeros_like(kw)
    q = q_ref[0]
    out = None
    for hi, keep in enumerate((low, ~low)):
        s = lax.dot_general(q, jnp.where(keep, kw, zero), _NT, preferred_element_type=F32) + t_ref[0, hi]
        p = jnp.exp(s - jnp.max(s, axis=1, keepdims=True))
        l = jnp.sum(p, axis=1, keepdims=True)
        o = jnp.dot(p.astype(BF16), jnp.where(keep, vw, zero), preferred_element_type=F32) / l
        out = o if out is None else out + o
    o_ref[0] = out.astype(BF16)


def na_attn(proj, rpb, col0):
    b, s, _ = proj.shape
    rows = s // GRID_W
    assert rows % NA_GROUP == 0 and rows >= 2 * NA_WIN
    groups = rows // NA_GROUP
    pairs = N_NA_HEADS // 2
    cb = col0 // LANES
    tables = _na_bias_tables(rpb, rows)
    tq = NA_GROUP * GRID_W

    def case(g):
        return jnp.where(g == 0, 0, jnp.where(g == groups - 1, 2, 1))

    return pl.pallas_call(
        functools.partial(_na_attn_kernel, rows=rows),
        grid=(pairs, b, groups),
        in_specs=[pl.BlockSpec((1, tq, LANES), lambda p, bi, g: (bi, g, cb + p)),
                  pl.BlockSpec((1, s, LANES), lambda p, bi, g: (bi, 0, cb + pairs + p)),
                  pl.BlockSpec((1, s, LANES), lambda p, bi, g: (bi, 0, cb + 2 * pairs + p)),
                  pl.BlockSpec((1, 2, tq, NA_WIN * GRID_W), lambda p, bi, g: (case(g), p, 0, 0))],
        out_specs=pl.BlockSpec((1, tq, LANES), lambda p, bi, g: (bi, g, p)),
        out_shape=jax.ShapeDtypeStruct((b, s, N_NA_HEADS * HEAD_DIM), BF16),
        compiler_params=pltpu.CompilerParams(
            dimension_semantics=("parallel", "parallel", "parallel"),
            vmem_limit_bytes=VMEM_LIMIT),
        name="na_attn")(proj, proj, proj, tables)


def _out_proj_kernel(*refs, n_x, first):
    a_ref, b_ref, wa_ref, wb_ref, o_ref = refs[n_x:]
    o_ref[...] = (_row_tile(refs[:n_x], first)
                  + jnp.dot(a_ref[...], wa_ref[...], preferred_element_type=F32)
                  + jnp.dot(b_ref[...], wb_ref[...], preferred_element_type=F32))


def out_proj(x_parts, oa, ob, wa, wb, tm=512):
    n, d = oa.shape[0], wa.shape[1]
    x_specs, first = _row_tile_specs(x_parts, tm)
    return pl.pallas_call(
        functools.partial(_out_proj_kernel, n_x=len(x_parts), first=first),
        grid=(n // tm,),
        in_specs=x_specs + [pl.BlockSpec((tm, oa.shape[1]), lambda i: (i, 0)),
                            pl.BlockSpec((tm, ob.shape[1]), lambda i: (i, 0)),
                            pl.BlockSpec(wa.shape, lambda i: (0, 0)),
                            pl.BlockSpec(wb.shape, lambda i: (0, 0))],
        out_specs=pl.BlockSpec((tm, d), lambda i: (i, 0)),
        out_shape=jax.ShapeDtypeStruct((n, d), F32),
        compiler_params=pltpu.CompilerParams(dimension_semantics=("parallel",),
                                             vmem_limit_bytes=VMEM_LIMIT),
        name="out_proj")(*x_parts, oa, ob, wa, wb)


def _swiglu_act(a, b):
    return (a * jax.nn.sigmoid(a) * b).astype(BF16)


def _ffn_dense_kernel(x_ref, g_ref, w1_ref, w3_ref, w2_ref, gf_ref, o_ref, *, chunk, final_norm):
    x = x_ref[...]
    h = _rms(x, g_ref[...]).astype(BF16)
    y = x
    for c in range(w1_ref.shape[1] // chunk):
        cols = slice(c * chunk, (c + 1) * chunk)
        act = _swiglu_act(jnp.dot(h, w1_ref[:, cols], preferred_element_type=F32),
                          jnp.dot(h, w3_ref[:, cols], preferred_element_type=F32))
        y = y + jnp.dot(act, w2_ref[cols, :], preferred_element_type=F32)
    o_ref[...] = _rms(y, gf_ref[...]) if final_norm else y


def ffn_dense(x, g, w1, w3, w2, g_final, final_norm, tm=512):
    n, d = x.shape
    f = w1.shape[1]
    chunk = f // 2 if (f // 2) % LANES == 0 else f
    resident = dict(pipeline_mode=pl.Buffered(1))
    return pl.pallas_call(
        functools.partial(_ffn_dense_kernel, chunk=chunk, final_norm=final_norm),
        grid=(n // tm,),
        in_specs=[pl.BlockSpec((tm, d), lambda i: (i, 0)),
                  pl.BlockSpec((1, d), lambda i: (0, 0)),
                  pl.BlockSpec((d, f), lambda i: (0, 0), **resident),
                  pl.BlockSpec((d, f), lambda i: (0, 0), **resident),
                  pl.BlockSpec((f, d), lambda i: (0, 0), **resident),
                  pl.BlockSpec((1, d), lambda i: (0, 0))],
        out_specs=pl.BlockSpec((tm, d), lambda i: (i, 0)),
        out_shape=jax.ShapeDtypeStruct((n, d), F32),
        compiler_params=pltpu.CompilerParams(dimension_semantics=("parallel",),
                                             vmem_limit_bytes=VMEM_LIMIT),
        name="ffn_dense")(x, g, w1, w3, w2, g_final)


MOE_TT = 512
MOE_CH = 16
MOE_R = 2 * MOE_TT + LANES
MOE_LCH = MOE_R // MOE_CH


def _moe_route_kernel(x_ref, g_ref, rt_ref, stage_ref, cnt_ref, info_ref, *, n_tiles):
    tt = x_ref.shape[0]
    ne = rt_ref.shape[0]

    @pl.when(pl.program_id(0) == n_tiles)
    def _zero_tile():
        stage_ref[...] = jnp.zeros(stage_ref.shape, BF16)
        cnt_ref[...] = jnp.zeros(cnt_ref.shape, F32)
        info_ref[...] = jnp.zeros(info_ref.shape, F32)

    @pl.when(pl.program_id(0) < n_tiles)
    def _route():
        h = _rms(x_ref[...], g_ref[...]).astype(BF16)
        logits = lax.dot_general(rt_ref[...], h, _NT, preferred_element_type=F32)
        erow = lax.broadcasted_iota(jnp.int32, (ne, tt), 0)
        v1 = jnp.max(logits, axis=0, keepdims=True)
        i1 = jnp.min(jnp.where(logits == v1, erow, ne), axis=0, keepdims=True)
        rest = jnp.where(erow == i1, -jnp.inf, logits)
        v2 = jnp.max(rest, axis=0, keepdims=True)
        i2 = jnp.min(jnp.where(rest == v2, erow, ne), axis=0, keepdims=True)
        e2 = jnp.exp(v2 - v1)
        g1 = 1.0 / (1.0 + e2)
        g2 = e2 / (1.0 + e2)
        member = jnp.where((erow == i1) | (erow == i2), 1.0, 0.0)
        before = (lax.broadcasted_iota(jnp.int32, (tt, tt), 0)
                  < lax.broadcasted_iota(jnp.int32, (tt, tt), 1))
        cum = jnp.dot(member.astype(BF16), jnp.where(before, 1.0, 0.0).astype(BF16),
                      preferred_element_type=F32)
        cnt = jnp.sum(member, axis=1, keepdims=True)
        padded = jnp.floor((cnt + (MOE_CH - 1)) * (1.0 / MOE_CH)) * MOE_CH
        padded_b = jnp.broadcast_to(padded, (ne, tt))
        offset = jnp.zeros((1, tt), F32)
        rank1 = jnp.zeros((1, tt), F32)
        rank2 = jnp.zeros((1, tt), F32)
        for e in range(ne):
            pos = offset + cum[e:e + 1]
            rank1 = rank1 + jnp.where(i1 == e, pos, 0.0)
            rank2 = rank2 + jnp.where(i2 == e, pos, 0.0)
            offset = offset + padded_b[e:e + 1]
        r = lax.broadcasted_iota(jnp.int32, (MOE_R, tt), 0).astype(F32)
        perm = jnp.where((r == rank1) | (r == rank2), 1.0, 0.0).astype(BF16)
        stage_ref[0] = jnp.dot(perm, h, preferred_element_type=F32).astype(BF16)
        cnt_ref[0] = jnp.broadcast_to(padded, (ne, LANES))
        irow = lax.broadcasted_iota(jnp.int32, (8, tt), 0)
        info_ref[0] = jnp.where(irow == 0, rank1, jnp.where(irow == 1, rank2,
                                jnp.where(irow == 2, g1, jnp.where(irow == 3, g2, 0.0))))


def moe_route(x, g, router_t):
    n, d = x.shape
    ne = router_t.shape[0]
    assert ne == 8 and n % MOE_TT == 0
    nt = n // MOE_TT
    return pl.pallas_call(
        functools.partial(_moe_route_kernel, n_tiles=nt),
        grid=(nt + 1,),
        in_specs=[pl.BlockSpec((MOE_TT, d), lambda i: (jnp.minimum(i, nt - 1), 0)),
                  pl.BlockSpec((1, d), lambda i: (0, 0)),
                  pl.BlockSpec((ne, d), lambda i: (0, 0))],
        out_specs=[pl.BlockSpec((1, MOE_R, d), lambda i: (i, 0, 0)),
                   pl.BlockSpec((1, ne, LANES), lambda i: (i, 0, 0)),
                   pl.BlockSpec((1, 8, MOE_TT), lambda i: (i, 0, 0))],
        out_shape=[jax.ShapeDtypeStruct((nt + 1, MOE_R, d), BF16),
                   jax.ShapeDtypeStruct((nt + 1, ne, LANES), F32),
                   jax.ShapeDtypeStruct((nt + 1, 8, MOE_TT), F32)],
        compiler_params=pltpu.CompilerParams(dimension_semantics=("arbitrary",),
                                             vmem_limit_bytes=VMEM_LIMIT),
        name="moe_route")(x, g, router_t)


def _moe_tables(padded_counts, tm):
    nt, ne = padded_counts.shape
    tile_ch = tm // MOE_CH
    lc = padded_counts // MOE_CH
    loc = jnp.cumsum(lc, axis=1) - lc
    per_expert = jnp.sum(lc, axis=0)
    per_expert_pad = (per_expert + tile_ch - 1) // tile_ch * tile_ch
    expert_end = jnp.cumsum(per_expert_pad)
    expert_base = expert_end - per_expert_pad
    seg_dst = expert_base[None, :] + jnp.cumsum(lc, axis=0) - lc
    max_chunks = (2 * nt * MOE_TT) // MOE_CH + nt * ne + ne * (tile_ch - 1)
    n_tiles = -(-max_chunks // tile_ch)
    n_active = (expert_end[-1] // tile_ch).astype(jnp.int32)
    tile_expert = jnp.minimum(jnp.sum(jnp.arange(n_tiles)[:, None] >= (expert_end // tile_ch)[None, :], axis=1),
                              ne - 1).astype(jnp.int32)
    start = seg_dst.reshape(-1)
    length = lc.reshape(-1)
    source = (jnp.arange(nt)[:, None] * MOE_LCH + loc).reshape(-1)
    d = jnp.arange(n_tiles * tile_ch)[:, None]
    hit = (d >= start[None, :]) & (d < (start + length)[None, :])
    zero_chunk = nt * MOE_LCH
    ffn_src = jnp.sum(jnp.where(hit, (source - start)[None, :] + d, 0), axis=1)
    ffn_src = jnp.where(jnp.any(hit, axis=1), ffn_src, zero_chunk).astype(jnp.int32)
    l = jnp.arange(MOE_LCH)[None, :, None]
    own = (l >= loc[:, None, :]) & (l < (loc + lc)[:, None, :])
    cmb = jnp.sum(jnp.where(own, (seg_dst - loc)[:, None, :] + l, 0), axis=2).astype(jnp.int32)
    n_valid = jnp.sum(lc, axis=1).astype(jnp.int32)
    return tile_expert, n_active.reshape(1), ffn_src, cmb.reshape(-1), n_valid, n_tiles


def _chunk_copies(table_ref, first, count, src_hbm, dst_ref, sem, start):
    def body(c, carry):
        row = pl.multiple_of(table_ref[first + c] * MOE_CH, MOE_CH)
        copy = pltpu.make_async_copy(src_hbm.at[pl.ds(row, MOE_CH), :],
                                     dst_ref.at[pl.ds(pl.multiple_of(c * MOE_CH, MOE_CH), MOE_CH), :], sem)
        if start:
            copy.start()
        else:
            copy.wait()
        return carry
    lax.fori_loop(0, count, body, 0)


def _moe_ffn_kernel(te_ref, nact_ref, src_ref, stage_hbm, w1_ref, w3_ref, w2_ref, o_ref,
                    xbuf, sem, acc_ref, *, tm):
    i = pl.program_id(0)
    j = pl.program_id(1)
    nf = pl.num_programs(1)
    nch = tm // MOE_CH
    nact = nact_ref[0]
    slot = i % 2

    @pl.when((i == 0) & (j == 0))
    def _prime():
        _chunk_copies(src_ref, 0, nch, stage_hbm, xbuf.at[0], sem.at[0], True)

    @pl.when((i < nact) & (j == 0))
    def _rotate():
        _chunk_copies(src_ref, i * nch, nch, stage_hbm, xbuf.at[slot], sem.at[slot], False)

        @pl.when(i + 1 < nact)
        def _prefetch():
            _chunk_copies(src_ref, (i + 1) * nch, nch, stage_hbm, xbuf.at[1 - slot], sem.at[1 - slot], True)

    @pl.when(i < nact)
    def _compute():
        x = xbuf[slot]
        act = _swiglu_act(jnp.dot(x, w1_ref[0], preferred_element_type=F32),
                          jnp.dot(x, w3_ref[0], preferred_element_type=F32))
        part = jnp.dot(act, w2_ref[0], preferred_element_type=F32)

        @pl.when(j == 0)
        def _first():
            acc_ref[...] = part

        @pl.when(j > 0)
        def _rest():
            acc_ref[...] += part

        @pl.when(j == nf - 1)
        def _store():
            o_ref[...] = acc_ref[...].astype(BF16)

    @pl.when((i >= nact) & (j == nf - 1))
    def _unused_tile():
        o_ref[...] = jnp.zeros(o_ref.shape, BF16)


def moe_ffn(stage, tile_expert, n_active, ffn_src, w1, w3, w2, n_tiles, tm):
    d = stage.shape[-1]
    f = w1.shape[2]
    tf = next((c for c in (7 * LANES, 4 * LANES) if f % c == 0), f)
    nf = f // tf

    def live(i, na):
        return jnp.minimum(i, na[0] - 1)

    def fcol(i, j, na):
        return jnp.where(i < na[0], j, nf - 1)

    grid_spec = pltpu.PrefetchScalarGridSpec(
        num_scalar_prefetch=3,
        grid=(n_tiles, nf),
        in_specs=[pl.BlockSpec(memory_space=pl.ANY),
                  pl.BlockSpec((1, d, tf), lambda i, j, te, na, src: (te[live(i, na)], 0, fcol(i, j, na))),
                  pl.BlockSpec((1, d, tf), lambda i, j, te, na, src: (te[live(i, na)], 0, fcol(i, j, na))),
                  pl.BlockSpec((1, tf, d), lambda i, j, te, na, src: (te[live(i, na)], fcol(i, j, na), 0))],
        out_specs=pl.BlockSpec((tm, d), lambda i, j, te, na, src: (i, 0)),
        scratch_shapes=[pltpu.VMEM((2, tm, d), BF16),
                        pltpu.SemaphoreType.DMA((2,)),
                        pltpu.VMEM((tm, d), F32)])
    return pl.pallas_call(
        functools.partial(_moe_ffn_kernel, tm=tm),
        grid_spec=grid_spec,
        out_shape=jax.ShapeDtypeStruct((n_tiles * tm, d), BF16),
        compiler_params=pltpu.CompilerParams(dimension_semantics=("arbitrary", "arbitrary"),
                                             vmem_limit_bytes=VMEM_LIMIT),
        name="moe_ffn")(tile_expert, n_active, ffn_src, stage.reshape(-1, d), w1, w3, w2)


def _moe_combine_kernel(cmb_ref, nval_ref, x_ref, info_ref, ys_hbm, gf_ref, *refs, first, final_norm):
    o_refs, (ybuf, sem) = refs[:-2], refs[-2:]
    t = pl.program_id(0)
    nt = pl.num_programs(0)
    slot = t % 2

    def fetch(tile, slot, start):
        nv = nval_ref[tile]
        _chunk_copies(cmb_ref, tile * MOE_LCH, nv, ys_hbm, ybuf.at[slot], sem.at[slot], start)
        if start:
            def clear(l, carry):
                ybuf[slot, pl.ds(pl.multiple_of(l * MOE_CH, MOE_CH), MOE_CH), :] = jnp.zeros(
                    (MOE_CH, ybuf.shape[2]), BF16)
                return carry
            lax.fori_loop(nv, MOE_LCH, clear, 0)

    @pl.when(t == 0)
    def _prime():
        fetch(0, 0, True)

    fetch(t, slot, False)

    @pl.when(t + 1 < nt)
    def _prefetch():
        fetch(t + 1, 1 - slot, True)

    info = info_ref[...]
    tt = info.shape[0]
    col = lax.broadcasted_iota(jnp.int32, (tt, MOE_R), 1).astype(F32)
    pick = jnp.concatenate([jnp.where(col == info[:, 0:1], 1.0, 0.0).astype(BF16),
                            jnp.where(col == info[:, 1:2], 1.0, 0.0).astype(BF16)], axis=0)
    y = jnp.dot(pick, ybuf[slot], preferred_element_type=F32)
    out = x_ref[...] + info[:, 2:3] * y[:tt] + info[:, 3:4] * y[tt:]
    out = _rms(out, gf_ref[...]) if final_norm else out
    if len(o_refs) == 1:
        o_refs[0][...] = out
    else:
        @pl.when(t < first)
        def _head():
            o_refs[0][...] = out

        @pl.when(t >= first)
        def _tail():
            o_refs[1][...] = out


def moe_combine(x, info_cols, ys, cmb, n_valid, g_final, final_norm, row_split):
    n, d = x.shape
    nt = n // MOE_TT
    assert sum(row_split) == n and all(r % MOE_TT == 0 for r in row_split) and len(row_split) <= 2
    first = row_split[0] // MOE_TT
    if len(row_split) == 1:
        out_specs = [pl.BlockSpec((MOE_TT, d), lambda t, c, v: (t, 0))]
    else:
        out_specs = [pl.BlockSpec((MOE_TT, d), lambda t, c, v: (jnp.minimum(t, first - 1), 0)),
                     pl.BlockSpec((MOE_TT, d), lambda t, c, v: (jnp.maximum(t - first, 0), 0))]
    grid_spec = pltpu.PrefetchScalarGridSpec(
        num_scalar_prefetch=2,
        grid=(nt,),
        in_specs=[pl.BlockSpec((MOE_TT, d), lambda t, c, v: (t, 0)),
                  pl.BlockSpec((MOE_TT, 8), lambda t, c, v: (t, 0)),
                  pl.BlockSpec(memory_space=pl.ANY),
                  pl.BlockSpec((1, d), lambda t, c, v: (0, 0))],
        out_specs=out_specs,
        scratch_shapes=[pltpu.VMEM((2, MOE_R, d), BF16),
                        pltpu.SemaphoreType.DMA((2,))])
    return pl.pallas_call(
        functools.partial(_moe_combine_kernel, first=first, final_norm=final_norm),
        grid_spec=grid_spec,
        out_shape=[jax.ShapeDtypeStruct((r, d), F32) for r in row_split],
        compiler_params=pltpu.CompilerParams(dimension_semantics=("arbitrary",),
                                             vmem_limit_bytes=VMEM_LIMIT),
        name="moe_combine")(cmb, n_valid, x, info_cols, ys, g_final)


def moe_top2(x, g, router, w1, w3, w2, g_final, final_norm, row_split, tm=1024):
    n, d = x.shape
    nt = n // MOE_TT
    stage, counts, info = moe_route(x, g, router.T.astype(BF16))
    padded_counts = counts[:nt, :, 0].astype(jnp.int32)
    tile_expert, n_active, ffn_src, cmb, n_valid, n_tiles = _moe_tables(padded_counts, tm)
    ys = moe_ffn(stage, tile_expert, n_active, ffn_src, w1, w3, w2, n_tiles, tm)
    info_cols = info[:nt].transpose(0, 2, 1).reshape(n, 8)
    return moe_combine(x, info_cols, ys, cmb, n_valid, g_final, final_norm, row_split)


def _trunk(x_parts, w_in, w_out, norm_mix, norm_ffn, lambda_qk, subln_g, na_rpb,
           ffn_w1, ffn_w3, ffn_w2, moe_router, moe_w1, moe_w3, moe_w2, norm_final):
    s, d = x_parts[0].shape[1:]
    rows = tuple(p.shape[0] * s for p in x_parts)
    b, n = sum(p.shape[0] for p in x_parts), sum(rows)
    depth = w_in.shape[0]
    diff_w = N_DIFF_HEADS * 2 * HEAD_DIM
    na_w = N_NA_HEADS * HEAD_DIM
    scale = HEAD_DIM ** -0.5
    col_scale = jnp.ones((w_in.shape[2],), F32)
    col_scale = col_scale.at[:diff_w].set(scale).at[3 * diff_w:3 * diff_w + na_w].set(scale)
    xs = [p.reshape(-1, d) for p in x_parts]
    for l in range(depth):
        proj = norm_proj(xs, norm_mix[l].reshape(1, d), (w_in[l] * col_scale).astype(BF16))
        proj = proj.reshape(b, s, -1)
        oa = diff_attn(proj, lambda_qk[l], subln_g[l], l).reshape(n, diff_w)
        ob = na_attn(proj, na_rpb[l], 3 * diff_w).reshape(n, na_w)
        wo = w_out[l].astype(BF16)
        x = out_proj(xs, oa, ob, wo[:diff_w], wo[diff_w:])
        last = l == depth - 1
        i = l // 2
        if l % 2 == 0:
            xs = [ffn_dense(x, norm_ffn[l].reshape(1, d), ffn_w1[i].astype(BF16),
                            ffn_w3[i].astype(BF16), ffn_w2[i].astype(BF16),
                            norm_final.reshape(1, d), final_norm=last)]
        else:
            xs = moe_top2(x, norm_ffn[l].reshape(1, d), moe_router[i], moe_w1[i].astype(BF16),
                          moe_w3[i].astype(BF16), moe_w2[i].astype(BF16),
                          norm_final.reshape(1, d), final_norm=last,
                          row_split=rows if last else (n,))
    if len(xs) != len(rows):
        bounds = np.cumsum((0,) + rows)
        xs = [xs[0][lo:hi] for lo, hi in zip(bounds[:-1], bounds[1:])]
    return tuple(y.reshape(-1, s, d) for y in xs)


def kernel(x_prompt, x_sample, w_in, w_out, norm_mix, norm_ffn, lambda_qk, subln_g, na_rpb,
           ffn_w1, ffn_w3, ffn_w2, moe_router, moe_w1, moe_w3, moe_w2, norm_final):
    assert x_prompt.shape[1:] == x_sample.shape[1:]
    return _trunk([x_prompt, x_sample], w_in, w_out, norm_mix, norm_ffn, lambda_qk, subln_g, na_rpb,
                  ffn_w1, ffn_w3, ffn_w2, moe_router, moe_w1, moe_w3, moe_w2, norm_final)
```

```python
import functools
import math

import numpy as np
import jax
import jax.numpy as jnp
from jax import lax
from jax.experimental import pallas as pl
from jax.experimental.pallas import tpu as pltpu

F32 = jnp.float32
BF16 = jnp.bfloat16

RMS_EPS = 1e-5
HEAD_DIM = 64
N_DIFF_HEADS = 4
N_NA_HEADS = 8
GRID_W = 64
NA_KH = 8
NA_KW = 16
NA_GROUP = 8
NA_WIN = 16
TOP_K = 2
LANES = 128
NEG = -1e30
VMEM_LIMIT = 56 * 1024 * 1024

_NT = (((1,), (1,)), ((), ()))


def _rms(x, g):
    return x * lax.rsqrt(jnp.mean(x * x, axis=-1, keepdims=True) + RMS_EPS) * g


def _row_tile_specs(parts, tm):
    first = parts[0].shape[0] // tm
    d = parts[0].shape[1]
    specs = [pl.BlockSpec((tm, d), lambda i: (jnp.minimum(i, first - 1), 0))]
    if len(parts) == 2:
        assert parts[0].shape[0] % tm == 0 and parts[1].shape[0] % tm == 0
        specs.append(pl.BlockSpec((tm, d), lambda i: (jnp.maximum(i - first, 0), 0)))
    return specs, first


def _row_tile(x_refs, first):
    if len(x_refs) == 1:
        return x_refs[0][...]
    return jnp.where(pl.program_id(0) < first, x_refs[0][...], x_refs[1][...])


def _norm_proj_kernel(*refs, n_x, first, chunk):
    g_ref, w_ref, o_ref = refs[n_x:]
    h = _rms(_row_tile(refs[:n_x], first), g_ref[...]).astype(BF16)
    for c in range(o_ref.shape[1] // chunk):
        cols = slice(c * chunk, (c + 1) * chunk)
        o_ref[:, cols] = jnp.dot(h, w_ref[:, cols], preferred_element_type=F32).astype(BF16)


def norm_proj(x_parts, g, w, tm=512):
    n = sum(p.shape[0] for p in x_parts)
    d, c = w.shape
    x_specs, first = _row_tile_specs(x_parts, tm)
    return pl.pallas_call(
        functools.partial(_norm_proj_kernel, n_x=len(x_parts), first=first, chunk=1024),
        grid=(n // tm,),
        in_specs=x_specs + [pl.BlockSpec((1, d), lambda i: (0, 0)),
                            pl.BlockSpec((d, c), lambda i: (0, 0))],
        out_specs=pl.BlockSpec((tm, c), lambda i: (i, 0)),
        out_shape=jax.ShapeDtypeStruct((n, c), BF16),
        compiler_params=pltpu.CompilerParams(dimension_semantics=("parallel",),
                                             vmem_limit_bytes=VMEM_LIMIT),
        name="norm_proj")(*x_parts, g, w)


SCORE_CAP = 40.0
EXP_ZERO = 88.0


def _diff_attn_kernel(slopes_ref, q_ref, k_ref, v_ref, lam_ref, g_ref, o_ref,
                      k1_ref, k2_ref, va_ref, qv_ref, dist_ref, acc_ref, m_ref, kn_ref,
                      *, t, whole, lambda_init):
    h = pl.program_id(1)
    i = pl.program_id(2)
    n = k_ref.shape[1] // t
    slope = slopes_ref[h]
    lane = lax.broadcasted_iota(jnp.int32, (t, LANES), 1)
    low = lane < HEAD_DIM
    centred = (lax.broadcasted_iota(jnp.int32, (t, LANES), 0) - t // 2).astype(F32)

    def extras(base, a, b, c, d):
        return jnp.where(lane == base, a, jnp.where(lane == base + 1, b,
                         jnp.where(lane == base + 2, c, jnp.where(lane == base + 3, d, 0.0))))

    @pl.when(i == 0)
    def _prepare_keys():
        def body(j, norms):
            rows = pl.ds(pl.multiple_of(j * t, t), t)
            k = k_ref[0, rows, :].astype(F32)
            base = slope * jnp.asarray(j * t).astype(F32)
            k1_ref[rows, :] = jnp.where(low, k, extras(HEAD_DIM, -slope, slope * centred, 1.0, base)).astype(BF16)
            k2_ref[rows, :] = jnp.where(low, extras(0, -slope, slope * centred, 1.0, base), k).astype(BF16)
            va_ref[rows, :LANES] = v_ref[0, rows, :]
            va_ref[rows, LANES:] = jnp.ones((t, LANES), BF16)
            k2 = k * k
            n1 = jnp.sum(jnp.where(low, k2, 0.0), axis=1, keepdims=True)
            n2 = jnp.sum(jnp.where(low, 0.0, k2), axis=1, keepdims=True)
            return jnp.maximum(norms[0], n1), jnp.maximum(norms[1], n2)

        zero = jnp.zeros((t, 1), F32)
        n1, n2 = lax.fori_loop(0, n, body, (zero, zero))
        kn_ref[0] = jnp.max(n1)
        kn_ref[1] = jnp.max(n2)
        dist_ref[...] = slope * jnp.abs(lax.broadcasted_iota(jnp.int32, (t, t), 0)
                                        - lax.broadcasted_iota(jnp.int32, (t, t), 1)).astype(F32)

    q = q_ref[0].astype(F32)
    qbase = -slope * jnp.asarray(i * t).astype(F32)
    for si, sign in enumerate((1.0, 0.0, -1.0)):
        qv_ref[0, si] = jnp.where(low, q, sign * extras(HEAD_DIM, centred, 1.0, qbase, 1.0)).astype(BF16)
        qv_ref[1, si] = jnp.where(low, sign * extras(0, centred, 1.0, qbase, 1.0), q).astype(BF16)
    before_q, diag_q, after_q = 0, 1, 2

    q2 = q * q
    u2 = jnp.maximum(jnp.sum(jnp.where(low, q2, 0.0), axis=1, keepdims=True) * kn_ref[0],
                     jnp.sum(jnp.where(low, 0.0, q2), axis=1, keepdims=True) * kn_ref[1])
    u = 1.01 * jnp.sqrt(u2)
    unshifted = jnp.max(u) <= SCORE_CAP
    reach = jnp.max(jnp.floor((EXP_ZERO + 2.0 * u) / (slope * t)) + 1.0).astype(jnp.int32)
    reach = jnp.where(unshifted, reach, n)
    j_lo = jnp.maximum(i - reach, 0)
    j_hi = jnp.minimum(i + reach, n - 1)

    acc_ref[...] = jnp.zeros(acc_ref.shape, F32)

    def scores(mi, qi, rows):
        km_ref = k1_ref if mi == 0 else k2_ref
        s = lax.dot_general(qv_ref[mi, qi], km_ref[rows, :], _NT, preferred_element_type=F32)
        if qi == diag_q:
            s = s - dist_ref[...]
        return s

    def plain_tile(j, qi, width=1):
        rows = pl.ds(pl.multiple_of(j * t, t), width * t)
        va = va_ref[rows, :]
        for mi in range(2):
            p = jnp.exp(scores(mi, qi, rows)).astype(BF16)
            acc_ref[mi] += jnp.dot(p, va, preferred_element_type=F32)

    def online_tile(j, qi, width=1):
        rows = pl.ds(pl.multiple_of(j * t, t), width * t)
        va = va_ref[rows, :]
        for mi in range(2):
            s = scores(mi, qi, rows)
            m_old = m_ref[mi]
            m_new = jnp.maximum(m_old, jnp.max(s, axis=1, keepdims=True))
            p = jnp.exp(s - m_new)
            acc_ref[mi] = (jnp.exp(m_old - m_new) * acc_ref[mi]
                           + jnp.dot(p.astype(BF16), va, preferred_element_type=F32))
            m_ref[mi] = m_new

    def walk(tile, width):
        def run(lo, hi, qi):
            odd = (hi - lo) % width
            for r in range(width - 1):
                @pl.when(r < odd)
                def _single():
                    tile(lo + r, qi)

            def body(c, carry):
                tile(lo + odd + c * width, qi, width)
                return carry
            lax.fori_loop(0, (hi - lo) // width, body, 0)

        run(j_lo, i, before_q)
        tile(i, diag_q)
        run(i + 1, j_hi + 1, after_q)

    def whole_row_step(c, carry):
        for mi in range(2):
            km_ref = k1_ref if mi == 0 else k2_ref
            parts = []
            for w in range(whole):
                j = c * whole + w
                rows = pl.ds(pl.multiple_of(j * t, t), t)
                s = lax.dot_general(qv_ref[mi, jnp.clip(j - i, -1, 1) + 1], km_ref[rows, :], _NT,
                                    preferred_element_type=F32)
                s = s - jnp.where(j == i, 1.0, 0.0) * dist_ref[...]
                parts.append(jnp.dot(jnp.exp(s).astype(BF16), va_ref[rows, :], preferred_element_type=F32))
            acc_ref[mi] += sum(parts)
        return carry

    every_tile = (j_lo == 0) & (j_hi == n - 1)

    @pl.when(unshifted & every_tile)
    def _plain_whole_row():
        lax.fori_loop(0, n // whole, whole_row_step, 0)

    @pl.when(unshifted & jnp.logical_not(every_tile))
    def _plain():
        walk(plain_tile, 2)

    @pl.when(jnp.logical_not(unshifted))
    def _online():
        m_ref[...] = jnp.full(m_ref.shape, NEG, F32)
        walk(online_tile, 1)

    a1 = acc_ref[0]
    a2 = acc_ref[1]
    lf = lam_ref[...]
    lam_full = (jnp.exp(jnp.sum(lf[0:1] * lf[1:2], axis=1, keepdims=True))
                - jnp.exp(jnp.sum(lf[2:3] * lf[3:4], axis=1, keepdims=True)) + lambda_init)
    o = a1[:, :LANES] / a1[:, LANES:] - lam_full * (a2[:, :LANES] / a2[:, LANES:])
    o_ref[0] = (_rms(o, g_ref[...]) * (1.0 - lambda_init)).astype(BF16)


def diff_attn(proj, lam, subln_g, layer_idx, t=512):
    b, s, _ = proj.shape
    nh = N_DIFF_HEADS
    assert (8 % nh) == 0 and t <= 512 and (t & (t - 1)) == 0
    slopes = jnp.asarray([2.0 ** (-8.0 * (h + 1) / nh) for h in range(nh)], F32)
    lambda_init = 0.8 - 0.6 * math.exp(-0.3 * layer_idx)
    grid_spec = pltpu.PrefetchScalarGridSpec(
        num_scalar_prefetch=1,
        grid=(b, nh, s // t),
        in_specs=[pl.BlockSpec((1, t, LANES), lambda bi, h, i, sl: (bi, i, h)),
                  pl.BlockSpec((1, s, LANES), lambda bi, h, i, sl: (bi, 0, nh + h)),
                  pl.BlockSpec((1, s, LANES), lambda bi, h, i, sl: (bi, 0, 2 * nh + h)),
                  pl.BlockSpec((4, HEAD_DIM), lambda bi, h, i, sl: (0, 0)),
                  pl.BlockSpec((1, 2 * HEAD_DIM), lambda bi, h, i, sl: (0, 0))],
        out_specs=pl.BlockSpec((1, t, LANES), lambda bi, h, i, sl: (bi, i, h)),
        scratch_shapes=[pltpu.VMEM((s, LANES), BF16),
                        pltpu.VMEM((s, LANES), BF16),
                        pltpu.VMEM((s, 2 * LANES), BF16),
                        pltpu.VMEM((2, 3, t, LANES), BF16),
                        pltpu.VMEM((t, t), F32),
                        pltpu.VMEM((2, t, 2 * LANES), F32),
                        pltpu.VMEM((2, t, 1), F32),
                        pltpu.SMEM((2,), F32)])
    whole = math.gcd(s // t, 16)
    return pl.pallas_call(
        functools.partial(_diff_attn_kernel, t=t, whole=whole, lambda_init=lambda_init),
        grid_spec=grid_spec,
        out_shape=jax.ShapeDtypeStruct((b, s, nh * 2 * HEAD_DIM), BF16),
        compiler_params=pltpu.CompilerParams(
            dimension_semantics=("arbitrary", "arbitrary", "arbitrary"),
            vmem_limit_bytes=VMEM_LIMIT),
        name="diff_attn")(slopes, proj, proj, proj, lam, subln_g.reshape(1, -1))


def _na_window_start(group, rows):
    return jnp.clip(group * NA_GROUP - NA_KH // 2, 0, rows - NA_WIN)


def _na_bias_tables(rpb, rows):
    n_rho, n_chi = 2 * NA_KH - 1, 2 * NA_KW - 1
    c = np.arange(GRID_W)[:, None]
    kc = np.arange(GRID_W)[None, :]
    cs = np.clip(c - NA_KW // 2, 0, GRID_W - NA_KW)
    col_ok = (kc >= cs) & (kc < cs + NA_KW)
    col_sel = ((kc - c + NA_KW - 1)[..., None] == np.arange(n_chi)) & col_ok[..., None]
    qr = np.arange(NA_GROUP)[:, None]
    kr = np.arange(NA_WIN)[None, :]
    row_sel, row_ok = [], []
    for r0 in (0, NA_GROUP, rows - NA_GROUP):
        ws = min(max(r0 - NA_KH // 2, 0), rows - NA_WIN)
        r = r0 + qr
        rs = np.clip(r - NA_KH // 2, 0, rows - NA_KH)
        ok = (ws + kr >= rs) & (ws + kr < rs + NA_KH)
        row_ok.append(ok)
        row_sel.append(((ws + kr - r + NA_KH - 1)[..., None] == np.arange(n_rho)) & ok[..., None])
    row_sel, row_ok = np.stack(row_sel), np.stack(row_ok)
    by_col = jnp.einsum("hrd,ckd->hrck", rpb.astype(F32), col_sel.astype(np.float32), precision="highest")
    bias = jnp.einsum("hrck,gqnr->ghqcnk", by_col, row_sel.astype(np.float32), precision="highest")
    valid = row_ok[:, None, :, None, :, None] & col_ok[None, None, None, :, None, :]
    shape = (3, rpb.shape[0], NA_GROUP * GRID_W, NA_WIN * GRID_W)
    return jnp.where(valid, bias, NEG).reshape(shape)


def _na_attn_kernel(bmax_ref, q_ref, k_ref, v_ref, t_ref, o_ref, kn_ref, *, rows):
    g = pl.program_id(2)
    nwin = NA_WIN * GRID_W

    @pl.when(g == 0)
    def _key_norms():
        low_k = lax.broadcasted_iota(jnp.int32, (nwin, LANES), 1) < HEAD_DIM

        def body(c, norms):
            k = k_ref[0, pl.ds(pl.multiple_of(c * nwin, nwin), nwin), :].astype(F32)
            k2 = k * k
            return (jnp.maximum(norms[0], jnp.sum(jnp.where(low_k, k2, 0.0), axis=1, keepdims=True)),
                    jnp.maximum(norms[1], jnp.sum(jnp.where(low_k, 0.0, k2), axis=1, keepdims=True)))

        zero = jnp.zeros((nwin, 1), F32)
        n1, n2 = lax.fori_loop(0, k_ref.shape[1] // nwin, body, (zero, zero))
        kn_ref[0] = jnp.max(n1)
        kn_ref[1] = jnp.max(n2)

    start = pl.multiple_of(_na_window_start(g, rows) * GRID_W, 4 * GRID_W)
    win = pl.ds(start, nwin)
    kw = k_ref[0, win, :]
    vw = v_ref[0, win, :]
    low = lax.broadcasted_iota(jnp.int32, kw.shape, 1) < HEAD_DIM
    zero = jnp.zeros_like(kw)
    q = q_ref[0]
    q2 = q.astype(F32) * q.astype(F32)
    low_q = lax.broadcasted_iota(jnp.int32, q2.shape, 1) < HEAD_DIM
    u2 = jnp.maximum(jnp.sum(jnp.where(low_q, q2, 0.0), axis=1, keepdims=True) * kn_ref[0],
                     jnp.sum(jnp.where(low_q, 0.0, q2), axis=1, keepdims=True) * kn_ref[1])
    unshifted = jnp.max(1.01 * jnp.sqrt(u2)) + bmax_ref[0] <= SCORE_CAP

    @pl.when(unshifted)
    def _plain():
        ones = jnp.ones(vw.shape, BF16)
        out = None
        for hi, keep in enumerate((low, ~low)):
            s = lax.dot_general(q, jnp.where(keep, kw, zero), _NT, preferred_element_type=F32) + t_ref[0, hi]
            r = jnp.dot(jnp.exp(s).astype(BF16), jnp.concatenate([jnp.where(keep, vw, zero), ones], axis=1),
                        preferred_element_type=F32)
            o = r[:, :LANES] / r[:, LANES:]
            out = o if out is None else out + o
        o_ref[0] = out.astype(BF16)

    @pl.when(jnp.logical_not(unshifted))
    def _shifted():
        out = None
        for hi, keep in enumerate((low, ~low)):
            s = lax.dot_general(q, jnp.where(keep, kw, zero), _NT, preferred_element_type=F32) + t_ref[0, hi]
            p = jnp.exp(s - jnp.max(s, axis=1, keepdims=True))
            l = jnp.sum(p, axis=1, keepdims=True)
            o = jnp.dot(p.astype(BF16), jnp.where(keep, vw, zero), preferred_element_type=F32) / l
            out = o if out is None else out + o
        o_ref[0] = out.astype(BF16)


def na_attn(proj, rpb, col0):
    b, s, _ = proj.shape
    rows = s // GRID_W
    assert rows % NA_WIN == 0 and rows >= 2 * NA_WIN
    groups = rows // NA_GROUP
    pairs = N_NA_HEADS // 2
    cb = col0 // LANES
    tables = _na_bias_tables(rpb, rows)
    tq = NA_GROUP * GRID_W

    def case(g):
        return jnp.where(g == 0, 0, jnp.where(g == groups - 1, 2, 1))

    grid_spec = pltpu.PrefetchScalarGridSpec(
        num_scalar_prefetch=1,
        grid=(pairs, b, groups),
        in_specs=[pl.BlockSpec((1, tq, LANES), lambda p, bi, g, bm: (bi, g, cb + p)),
                  pl.BlockSpec((1, s, LANES), lambda p, bi, g, bm: (bi, 0, cb + pairs + p)),
                  pl.BlockSpec((1, s, LANES), lambda p, bi, g, bm: (bi, 0, cb + 2 * pairs + p)),
                  pl.BlockSpec((1, 2, tq, NA_WIN * GRID_W), lambda p, bi, g, bm: (case(g), p, 0, 0))],
        out_specs=pl.BlockSpec((1, tq, LANES), lambda p, bi, g, bm: (bi, g, p)),
        scratch_shapes=[pltpu.SMEM((2,), F32)])
    return pl.pallas_call(
        functools.partial(_na_attn_kernel, rows=rows),
        grid_spec=grid_spec,
        out_shape=jax.ShapeDtypeStruct((b, s, N_NA_HEADS * HEAD_DIM), BF16),
        compiler_params=pltpu.CompilerParams(
            dimension_semantics=("arbitrary", "arbitrary", "arbitrary"),
            vmem_limit_bytes=VMEM_LIMIT),
        name="na_attn")(jnp.max(jnp.abs(rpb)).astype(F32).reshape(1), proj, proj, proj, tables)


def _out_proj_kernel(*refs, n_x, first):
    a_ref, b_ref, wa_ref, wb_ref, o_ref = refs[n_x:]
    o_ref[...] = (_row_tile(refs[:n_x], first)
                  + jnp.dot(a_ref[...], wa_ref[...], preferred_element_type=F32)
                  + jnp.dot(b_ref[...], wb_ref[...], preferred_element_type=F32))


def out_proj(x_parts, oa, ob, wa, wb, tm=512):
    n, d = oa.shape[0], wa.shape[1]
    x_specs, first = _row_tile_specs(x_parts, tm)
    return pl.pallas_call(
        functools.partial(_out_proj_kernel, n_x=len(x_parts), first=first),
        grid=(n // tm,),
        in_specs=x_specs + [pl.BlockSpec((tm, oa.shape[1]), lambda i: (i, 0)),
                            pl.BlockSpec((tm, ob.shape[1]), lambda i: (i, 0)),
                            pl.BlockSpec(wa.shape, lambda i: (0, 0)),
                            pl.BlockSpec(wb.shape, lambda i: (0, 0))],
        out_specs=pl.BlockSpec((tm, d), lambda i: (i, 0)),
        out_shape=jax.ShapeDtypeStruct((n, d), F32),
        compiler_params=pltpu.CompilerParams(dimension_semantics=("parallel",),
                                             vmem_limit_bytes=VMEM_LIMIT),
        name="out_proj")(*x_parts, oa, ob, wa, wb)


def _swiglu_act(a, b):
    return (a * jax.nn.sigmoid(a) * b).astype(BF16)


def _ffn_dense_kernel(x_ref, g_ref, w1_ref, w3_ref, w2_ref, gf_ref, o_ref, *, chunk, final_norm):
    x = x_ref[...]
    h = _rms(x, g_ref[...]).astype(BF16)
    y = x
    for c in range(w1_ref.shape[1] // chunk):
        cols = slice(c * chunk, (c + 1) * chunk)
        act = _swiglu_act(jnp.dot(h, w1_ref[:, cols], preferred_element_type=F32),
                          jnp.dot(h, w3_ref[:, cols], preferred_element_type=F32))
        y = y + jnp.dot(act, w2_ref[cols, :], preferred_element_type=F32)
    o_ref[...] = _rms(y, gf_ref[...]) if final_norm else y


def ffn_dense(x, g, w1, w3, w2, g_final, final_norm, tm=512):
    n, d = x.shape
    f = w1.shape[1]
    chunk = f // 2 if (f // 2) % LANES == 0 else f
    resident = dict(pipeline_mode=pl.Buffered(1))
    return pl.pallas_call(
        functools.partial(_ffn_dense_kernel, chunk=chunk, final_norm=final_norm),
        grid=(n // tm,),
        in_specs=[pl.BlockSpec((tm, d), lambda i: (i, 0)),
                  pl.BlockSpec((1, d), lambda i: (0, 0)),
                  pl.BlockSpec((d, f), lambda i: (0, 0), **resident),
                  pl.BlockSpec((d, f), lambda i: (0, 0), **resident),
                  pl.BlockSpec((f, d), lambda i: (0, 0), **resident),
                  pl.BlockSpec((1, d), lambda i: (0, 0))],
        out_specs=pl.BlockSpec((tm, d), lambda i: (i, 0)),
        out_shape=jax.ShapeDtypeStruct((n, d), F32),
        compiler_params=pltpu.CompilerParams(dimension_semantics=("parallel",),
                                             vmem_limit_bytes=VMEM_LIMIT),
        name="ffn_dense")(x, g, w1, w3, w2, g_final)


MOE_TT = 512
MOE_CH = 16
MOE_R = 2 * MOE_TT + LANES
MOE_LCH = MOE_R // MOE_CH


def _moe_route_kernel(x_ref, g_ref, rt_ref, stage_ref, cnt_ref, info_ref, *, n_tiles):
    tt = x_ref.shape[0]
    ne = rt_ref.shape[0]

    @pl.when(pl.program_id(0) == n_tiles)
    def _zero_tile():
        stage_ref[...] = jnp.zeros(stage_ref.shape, BF16)
        cnt_ref[...] = jnp.zeros(cnt_ref.shape, F32)
        info_ref[...] = jnp.zeros(info_ref.shape, F32)

    @pl.when(pl.program_id(0) < n_tiles)
    def _route():
        h = _rms(x_ref[...], g_ref[...]).astype(BF16)
        logits = lax.dot_general(rt_ref[...], h, _NT, preferred_element_type=F32)
        erow = lax.broadcasted_iota(jnp.int32, (ne, tt), 0)
        v1 = jnp.max(logits, axis=0, keepdims=True)
        i1 = jnp.min(jnp.where(logits == v1, erow, ne), axis=0, keepdims=True)
        rest = jnp.where(erow == i1, -jnp.inf, logits)
        v2 = jnp.max(rest, axis=0, keepdims=True)
        i2 = jnp.min(jnp.where(rest == v2, erow, ne), axis=0, keepdims=True)
        e2 = jnp.exp(v2 - v1)
        g1 = 1.0 / (1.0 + e2)
        g2 = e2 / (1.0 + e2)
        member = jnp.where((erow == i1) | (erow == i2), 1.0, 0.0)
        before = (lax.broadcasted_iota(jnp.int32, (tt, tt), 0)
                  < lax.broadcasted_iota(jnp.int32, (tt, tt), 1))
        cum = jnp.dot(member.astype(BF16), jnp.where(before, 1.0, 0.0).astype(BF16),
                      preferred_element_type=F32)
        cnt = jnp.sum(member, axis=1, keepdims=True)
        padded = jnp.floor((cnt + (MOE_CH - 1)) * (1.0 / MOE_CH)) * MOE_CH
        padded_b = jnp.broadcast_to(padded, (ne, tt))
        offset = jnp.zeros((1, tt), F32)
        rank1 = jnp.zeros((1, tt), F32)
        rank2 = jnp.zeros((1, tt), F32)
        for e in range(ne):
            pos = offset + cum[e:e + 1]
            rank1 = rank1 + jnp.where(i1 == e, pos, 0.0)
            rank2 = rank2 + jnp.where(i2 == e, pos, 0.0)
            offset = offset + padded_b[e:e + 1]
        r = lax.broadcasted_iota(jnp.int32, (MOE_R, tt), 0).astype(F32)
        perm = jnp.where((r == rank1) | (r == rank2), 1.0, 0.0).astype(BF16)
        stage_ref[0] = jnp.dot(perm, h, preferred_element_type=F32).astype(BF16)
        cnt_ref[0] = jnp.broadcast_to(padded, (ne, LANES))
        irow = lax.broadcasted_iota(jnp.int32, (8, tt), 0)
        info_ref[0] = jnp.where(irow == 0, rank1, jnp.where(irow == 1, rank2,
                                jnp.where(irow == 2, g1, jnp.where(irow == 3, g2, 0.0))))


def moe_route(x, g, router_t):
    n, d = x.shape
    ne = router_t.shape[0]
    assert ne == 8 and n % MOE_TT == 0
    nt = n // MOE_TT
    return pl.pallas_call(
        functools.partial(_moe_route_kernel, n_tiles=nt),
        grid=(nt + 1,),
        in_specs=[pl.BlockSpec((MOE_TT, d), lambda i: (jnp.minimum(i, nt - 1), 0)),
                  pl.BlockSpec((1, d), lambda i: (0, 0)),
                  pl.BlockSpec((ne, d), lambda i: (0, 0))],
        out_specs=[pl.BlockSpec((1, MOE_R, d), lambda i: (i, 0, 0)),
                   pl.BlockSpec((1, ne, LANES), lambda i: (i, 0, 0)),
                   pl.BlockSpec((1, 8, MOE_TT), lambda i: (i, 0, 0))],
        out_shape=[jax.ShapeDtypeStruct((nt + 1, MOE_R, d), BF16),
                   jax.ShapeDtypeStruct((nt + 1, ne, LANES), F32),
                   jax.ShapeDtypeStruct((nt + 1, 8, MOE_TT), F32)],
        compiler_params=pltpu.CompilerParams(dimension_semantics=("arbitrary",),
                                             vmem_limit_bytes=VMEM_LIMIT),
        name="moe_route")(x, g, router_t)


def _moe_tables(padded_counts, tm):
    nt, ne = padded_counts.shape
    tile_ch = tm // MOE_CH
    lc = padded_counts // MOE_CH
    loc = jnp.cumsum(lc, axis=1) - lc
    per_expert = jnp.sum(lc, axis=0)
    per_expert_pad = (per_expert + tile_ch - 1) // tile_ch * tile_ch
    expert_end = jnp.cumsum(per_expert_pad)
    expert_base = expert_end - per_expert_pad
    seg_dst = expert_base[None, :] + jnp.cumsum(lc, axis=0) - lc
    max_chunks = (2 * nt * MOE_TT) // MOE_CH + nt * ne + ne * (tile_ch - 1)
    n_tiles = -(-max_chunks // tile_ch)
    n_active = (expert_end[-1] // tile_ch).astype(jnp.int32)
    tile_expert = jnp.minimum(jnp.sum(jnp.arange(n_tiles)[:, None] >= (expert_end // tile_ch)[None, :], axis=1),
                              ne - 1).astype(jnp.int32)
    start = seg_dst.reshape(-1)
    length = lc.reshape(-1)
    source = (jnp.arange(nt)[:, None] * MOE_LCH + loc).reshape(-1)
    d = jnp.arange(n_tiles * tile_ch)[:, None]
    hit = (d >= start[None, :]) & (d < (start + length)[None, :])
    zero_chunk = nt * MOE_LCH
    ffn_src = jnp.sum(jnp.where(hit, (source - start)[None, :] + d, 0), axis=1)
    ffn_src = jnp.where(jnp.any(hit, axis=1), ffn_src, zero_chunk).astype(jnp.int32)
    l = jnp.arange(MOE_LCH)[None, :, None]
    own = (l >= loc[:, None, :]) & (l < (loc + lc)[:, None, :])
    cmb = jnp.sum(jnp.where(own, (seg_dst - loc)[:, None, :] + l, 0), axis=2).astype(jnp.int32)
    n_valid = jnp.sum(lc, axis=1).astype(jnp.int32)
    return tile_expert, n_active.reshape(1), ffn_src, cmb.reshape(-1), n_valid, n_tiles


def _chunk_copies(table_ref, first, count, src_hbm, dst_ref, sem, start):
    def body(c, carry):
        row = pl.multiple_of(table_ref[first + c] * MOE_CH, MOE_CH)
        copy = pltpu.make_async_copy(src_hbm.at[pl.ds(row, MOE_CH), :],
                                     dst_ref.at[pl.ds(pl.multiple_of(c * MOE_CH, MOE_CH), MOE_CH), :], sem)
        if start:
            copy.start()
        else:
            copy.wait()
        return carry
    lax.fori_loop(0, count, body, 0)


def _moe_ffn_kernel(te_ref, nact_ref, src_ref, stage_hbm, w1_ref, w3_ref, w2_ref, o_ref,
                    xbuf, sem, acc_ref, *, tm):
    i = pl.program_id(0)
    j = pl.program_id(1)
    nf = pl.num_programs(1)
    nch = tm // MOE_CH
    nact = nact_ref[0]
    slot = i % 2

    @pl.when((i == 0) & (j == 0))
    def _prime():
        _chunk_copies(src_ref, 0, nch, stage_hbm, xbuf.at[0], sem.at[0], True)

    @pl.when((i < nact) & (j == 0))
    def _rotate():
        _chunk_copies(src_ref, i * nch, nch, stage_hbm, xbuf.at[slot], sem.at[slot], False)

        @pl.when(i + 1 < nact)
        def _prefetch():
            _chunk_copies(src_ref, (i + 1) * nch, nch, stage_hbm, xbuf.at[1 - slot], sem.at[1 - slot], True)

    @pl.when(i < nact)
    def _compute():
        @pl.when(j == 0)
        def _first():
            acc_ref[...] = jnp.zeros(acc_ref.shape, F32)

        x = xbuf[slot]
        tf = w1_ref.shape[2]
        sub = 2 * LANES if tf % (2 * LANES) == 0 else tf
        for c in range(tf // sub):
            cols = slice(c * sub, (c + 1) * sub)
            act = _swiglu_act(jnp.dot(x, w1_ref[0, :, cols], preferred_element_type=F32),
                              jnp.dot(x, w3_ref[0, :, cols], preferred_element_type=F32))
            acc_ref[...] += jnp.dot(act, w2_ref[0, cols, :], preferred_element_type=F32)

        @pl.when(j == nf - 1)
        def _store():
            o_ref[...] = acc_ref[...].astype(BF16)

    @pl.when((i >= nact) & (j == nf - 1))
    def _unused_tile():
        o_ref[...] = jnp.zeros(o_ref.shape, BF16)


def moe_ffn(stage, tile_expert, n_active, ffn_src, w1, w3, w2, n_tiles, tm):
    d = stage.shape[-1]
    f = w1.shape[2]
    tf = next((c for c in (14 * LANES, 4 * LANES) if f % c == 0), f)
    nf = f // tf

    def live(i, na):
        return jnp.minimum(i, na[0] - 1)

    def fcol(i, j, na):
        return jnp.where(i < na[0], j, nf - 1)

    grid_spec = pltpu.PrefetchScalarGridSpec(
        num_scalar_prefetch=3,
        grid=(n_tiles, nf),
        in_specs=[pl.BlockSpec(memory_space=pl.ANY),
                  pl.BlockSpec((1, d, tf), lambda i, j, te, na, src: (te[live(i, na)], 0, fcol(i, j, na))),
                  pl.BlockSpec((1, d, tf), lambda i, j, te, na, src: (te[live(i, na)], 0, fcol(i, j, na))),
                  pl.BlockSpec((1, tf, d), lambda i, j, te, na, src: (te[live(i, na)], fcol(i, j, na), 0))],
        out_specs=pl.BlockSpec((tm, d), lambda i, j, te, na, src: (i, 0)),
        scratch_shapes=[pltpu.VMEM((2, tm, d), BF16),
                        pltpu.SemaphoreType.DMA((2,)),
                        pltpu.VMEM((tm, d), F32)])
    return pl.pallas_call(
        functools.partial(_moe_ffn_kernel, tm=tm),
        grid_spec=grid_spec,
        out_shape=jax.ShapeDtypeStruct((n_tiles * tm, d), BF16),
        compiler_params=pltpu.CompilerParams(dimension_semantics=("arbitrary", "arbitrary"),
                                             vmem_limit_bytes=VMEM_LIMIT),
        name="moe_ffn")(tile_expert, n_active, ffn_src, stage.reshape(-1, d), w1, w3, w2)


def _moe_combine_kernel(cmb_ref, nval_ref, x_ref, info_ref, ys_hbm, gf_ref, *refs, first, final_norm):
    o_refs, (ybuf, sem) = refs[:-2], refs[-2:]
    t = pl.program_id(0)
    nt = pl.num_programs(0)
    slot = t % 2

    def fetch(tile, slot, start):
        nv = nval_ref[tile]
        _chunk_copies(cmb_ref, tile * MOE_LCH, nv, ys_hbm, ybuf.at[slot], sem.at[slot], start)
        if start:
            def clear(l, carry):
                ybuf[slot, pl.ds(pl.multiple_of(l * MOE_CH, MOE_CH), MOE_CH), :] = jnp.zeros(
                    (MOE_CH, ybuf.shape[2]), BF16)
                return carry
            lax.fori_loop(nv, MOE_LCH, clear, 0)

    @pl.when(t == 0)
    def _prime():
        fetch(0, 0, True)

    fetch(t, slot, False)

    @pl.when(t + 1 < nt)
    def _prefetch():
        fetch(t + 1, 1 - slot, True)

    info = info_ref[...]
    tt = info.shape[0]
    col = lax.broadcasted_iota(jnp.int32, (tt, MOE_R), 1).astype(F32)
    pick = jnp.concatenate([jnp.where(col == info[:, 0:1], 1.0, 0.0).astype(BF16),
                            jnp.where(col == info[:, 1:2], 1.0, 0.0).astype(BF16)], axis=0)
    y = jnp.dot(pick, ybuf[slot], preferred_element_type=F32)
    out = x_ref[...] + info[:, 2:3] * y[:tt] + info[:, 3:4] * y[tt:]
    out = _rms(out, gf_ref[...]) if final_norm else out
    if len(o_refs) == 1:
        o_refs[0][...] = out
    else:
        @pl.when(t < first)
        def _head():
            o_refs[0][...] = out

        @pl.when(t >= first)
        def _tail():
            o_refs[1][...] = out


def moe_combine(x, info_cols, ys, cmb, n_valid, g_final, final_norm, row_split):
    n, d = x.shape
    nt = n // MOE_TT
    assert sum(row_split) == n and all(r % MOE_TT == 0 for r in row_split) and len(row_split) <= 2
    first = row_split[0] // MOE_TT
    if len(row_split) == 1:
        out_specs = [pl.BlockSpec((MOE_TT, d), lambda t, c, v: (t, 0))]
    else:
        out_specs = [pl.BlockSpec((MOE_TT, d), lambda t, c, v: (jnp.minimum(t, first - 1), 0)),
                     pl.BlockSpec((MOE_TT, d), lambda t, c, v: (jnp.maximum(t - first, 0), 0))]
    grid_spec = pltpu.PrefetchScalarGridSpec(
        num_scalar_prefetch=2,
        grid=(nt,),
        in_specs=[pl.BlockSpec((MOE_TT, d), lambda t, c, v: (t, 0)),
                  pl.BlockSpec((MOE_TT, 8), lambda t, c, v: (t, 0)),
                  pl.BlockSpec(memory_space=pl.ANY),
                  pl.BlockSpec((1, d), lambda t, c, v: (0, 0))],
        out_specs=out_specs,
        scratch_shapes=[pltpu.VMEM((2, MOE_R, d), BF16),
                        pltpu.SemaphoreType.DMA((2,))])
    return pl.pallas_call(
        functools.partial(_moe_combine_kernel, first=first, final_norm=final_norm),
        grid_spec=grid_spec,
        out_shape=[jax.ShapeDtypeStruct((r, d), F32) for r in row_split],
        compiler_params=pltpu.CompilerParams(dimension_semantics=("arbitrary",),
                                             vmem_limit_bytes=VMEM_LIMIT),
        name="moe_combine")(cmb, n_valid, x, info_cols, ys, g_final)


def moe_top2(x, g, router, w1, w3, w2, g_final, final_norm, row_split, tm=1024):
    n, d = x.shape
    nt = n // MOE_TT
    stage, counts, info = moe_route(x, g, router.T.astype(BF16))
    padded_counts = counts[:nt, :, 0].astype(jnp.int32)
    tile_expert, n_active, ffn_src, cmb, n_valid, n_tiles = _moe_tables(padded_counts, tm)
    ys = moe_ffn(stage, tile_expert, n_active, ffn_src, w1, w3, w2, n_tiles, tm)
    info_cols = info[:nt].transpose(0, 2, 1).reshape(n, 8)
    return moe_combine(x, info_cols, ys, cmb, n_valid, g_final, final_norm, row_split)


def _trunk(x_parts, w_in, w_out, norm_mix, norm_ffn, lambda_qk, subln_g, na_rpb,
           ffn_w1, ffn_w3, ffn_w2, moe_router, moe_w1, moe_w3, moe_w2, norm_final):
    s, d = x_parts[0].shape[1:]
    rows = tuple(p.shape[0] * s for p in x_parts)
    b, n = sum(p.shape[0] for p in x_parts), sum(rows)
    depth = w_in.shape[0]
    diff_w = N_DIFF_HEADS * 2 * HEAD_DIM
    na_w = N_NA_HEADS * HEAD_DIM
    scale = HEAD_DIM ** -0.5
    col_scale = jnp.ones((w_in.shape[2],), F32)
    col_scale = col_scale.at[:diff_w].set(scale).at[3 * diff_w:3 * diff_w + na_w].set(scale)
    xs = [p.reshape(-1, d) for p in x_parts]
    for l in range(depth):
        proj = norm_proj(xs, norm_mix[l].reshape(1, d), (w_in[l] * col_scale).astype(BF16))
        proj = proj.reshape(b, s, -1)
        oa = diff_attn(proj, lambda_qk[l], subln_g[l], l).reshape(n, diff_w)
        ob = na_attn(proj, na_rpb[l], 3 * diff_w).reshape(n, na_w)
        wo = w_out[l].astype(BF16)
        x = out_proj(xs, oa, ob, wo[:diff_w], wo[diff_w:])
        last = l == depth - 1
        i = l // 2
        if l % 2 == 0:
            xs = [ffn_dense(x, norm_ffn[l].reshape(1, d), ffn_w1[i].astype(BF16),
                            ffn_w3[i].astype(BF16), ffn_w2[i].astype(BF16),
                            norm_final.reshape(1, d), final_norm=last)]
        else:
            xs = moe_top2(x, norm_ffn[l].reshape(1, d), moe_router[i], moe_w1[i].astype(BF16),
                          moe_w3[i].astype(BF16), moe_w2[i].astype(BF16),
                          norm_final.reshape(1, d), final_norm=last,
                          row_split=rows if last else (n,))
    if len(xs) != len(rows):
        bounds = np.cumsum((0,) + rows)
        xs = [xs[0][lo:hi] for lo, hi in zip(bounds[:-1], bounds[1:])]
    return tuple(y.reshape(-1, s, d) for y in xs)


def kernel(x_prompt, x_sample, w_in, w_out, norm_mix, norm_ffn, lambda_qk, subln_g, na_rpb,
           ffn_w1, ffn_w3, ffn_w2, moe_router, moe_w1, moe_w3, moe_w2, norm_final):
    assert x_prompt.shape[1:] == x_sample.shape[1:]
    return _trunk([x_prompt, x_sample], w_in, w_out, norm_mix, norm_ffn, lambda_qk, subln_g, na_rpb,
                  ffn_w1, ffn_w3, ffn_w2, moe_router, moe_w1, moe_w3, moe_w2, norm_final)
```

```python
import functools
import math

import numpy as np
import jax
import jax.numpy as jnp
from jax import lax
from jax.experimental import pallas as pl
from jax.experimental.pallas import tpu as pltpu

F32 = jnp.float32
BF16 = jnp.bfloat16

RMS_EPS = 1e-5
HEAD_DIM = 64
N_DIFF_HEADS = 4
N_NA_HEADS = 8
GRID_W = 64
NA_KH = 8
NA_KW = 16
NA_GROUP = 8
NA_WIN = 16
TOP_K = 2
LANES = 128
NEG = -1e30
VMEM_LIMIT = 56 * 1024 * 1024

_NT = (((1,), (1,)), ((), ()))


def _rms(x, g):
    return x * lax.rsqrt(jnp.mean(x * x, axis=-1, keepdims=True) + RMS_EPS) * g


def _row_tile_specs(parts, tm):
    first = parts[0].shape[0] // tm
    d = parts[0].shape[1]
    specs = [pl.BlockSpec((tm, d), lambda i: (jnp.minimum(i, first - 1), 0))]
    if len(parts) == 2:
        assert parts[0].shape[0] % tm == 0 and parts[1].shape[0] % tm == 0
        specs.append(pl.BlockSpec((tm, d), lambda i: (jnp.maximum(i - first, 0), 0)))
    return specs, first


def _row_tile(x_refs, first):
    if len(x_refs) == 1:
        return x_refs[0][...]
    return jnp.where(pl.program_id(0) < first, x_refs[0][...], x_refs[1][...])


def _norm_proj_kernel(*refs, n_x, first, chunk):
    g_ref, w_ref, o_ref = refs[n_x:]
    h = _rms(_row_tile(refs[:n_x], first), g_ref[...]).astype(BF16)
    for c in range(o_ref.shape[1] // chunk):
        cols = slice(c * chunk, (c + 1) * chunk)
        o_ref[:, cols] = jnp.dot(h, w_ref[:, cols], preferred_element_type=F32).astype(BF16)


def norm_proj(x_parts, g, w, tm=512):
    n = sum(p.shape[0] for p in x_parts)
    d, c = w.shape
    x_specs, first = _row_tile_specs(x_parts, tm)
    return pl.pallas_call(
        functools.partial(_norm_proj_kernel, n_x=len(x_parts), first=first, chunk=1024),
        grid=(n // tm,),
        in_specs=x_specs + [pl.BlockSpec((1, d), lambda i: (0, 0)),
                            pl.BlockSpec((d, c), lambda i: (0, 0))],
        out_specs=pl.BlockSpec((tm, c), lambda i: (i, 0)),
        out_shape=jax.ShapeDtypeStruct((n, c), BF16),
        compiler_params=pltpu.CompilerParams(dimension_semantics=("parallel",),
                                             vmem_limit_bytes=VMEM_LIMIT),
        name="norm_proj")(*x_parts, g, w)


SCORE_CAP = 40.0
EXP_ZERO = 88.0


def _diff_attn_kernel(slopes_ref, q_ref, k_ref, v_ref, lam_ref, g_ref, o_ref,
                      k1_ref, k2_ref, va_ref, qv_ref, dist_ref, acc_ref, m_ref, kn_ref,
                      *, t, whole, lambda_init):
    h = pl.program_id(1)
    i = pl.program_id(2)
    n = k_ref.shape[1] // t
    slope = slopes_ref[h]
    lane = lax.broadcasted_iota(jnp.int32, (t, LANES), 1)
    low = lane < HEAD_DIM
    centred = (lax.broadcasted_iota(jnp.int32, (t, LANES), 0) - t // 2).astype(F32)

    def extras(base, a, b, c, d):
        return jnp.where(lane == base, a, jnp.where(lane == base + 1, b,
                         jnp.where(lane == base + 2, c, jnp.where(lane == base + 3, d, 0.0))))

    @pl.when(i == 0)
    def _prepare_keys():
        def body(j, norms):
            rows = pl.ds(pl.multiple_of(j * t, t), t)
            k = k_ref[0, rows, :].astype(F32)
            base = slope * jnp.asarray(j * t).astype(F32)
            k1_ref[rows, :] = jnp.where(low, k, extras(HEAD_DIM, -slope, slope * centred, 1.0, base)).astype(BF16)
            k2_ref[rows, :] = jnp.where(low, extras(0, -slope, slope * centred, 1.0, base), k).astype(BF16)
            va_ref[rows, :LANES] = v_ref[0, rows, :]
            va_ref[rows, LANES:] = jnp.ones((t, LANES), BF16)
            k2 = k * k
            n1 = jnp.sum(jnp.where(low, k2, 0.0), axis=1, keepdims=True)
            n2 = jnp.sum(jnp.where(low, 0.0, k2), axis=1, keepdims=True)
            return jnp.maximum(norms[0], n1), jnp.maximum(norms[1], n2)

        zero = jnp.zeros((t, 1), F32)
        n1, n2 = lax.fori_loop(0, n, body, (zero, zero))
        kn_ref[0] = jnp.max(n1)
        kn_ref[1] = jnp.max(n2)
        dist_ref[...] = slope * jnp.abs(lax.broadcasted_iota(jnp.int32, (t, t), 0)
                                        - lax.broadcasted_iota(jnp.int32, (t, t), 1)).astype(F32)

    q = q_ref[0].astype(F32)
    qbase = -slope * jnp.asarray(i * t).astype(F32)
    for si, sign in enumerate((1.0, 0.0, -1.0)):
        qv_ref[0, si] = jnp.where(low, q, sign * extras(HEAD_DIM, centred, 1.0, qbase, 1.0)).astype(BF16)
        qv_ref[1, si] = jnp.where(low, sign * extras(0, centred, 1.0, qbase, 1.0), q).astype(BF16)
    before_q, diag_q, after_q = 0, 1, 2

    q2 = q * q
    u2 = jnp.maximum(jnp.sum(jnp.where(low, q2, 0.0), axis=1, keepdims=True) * kn_ref[0],
                     jnp.sum(jnp.where(low, 0.0, q2), axis=1, keepdims=True) * kn_ref[1])
    u = 1.01 * jnp.sqrt(u2)
    unshifted = jnp.max(u) <= SCORE_CAP
    reach = jnp.max(jnp.floor((EXP_ZERO + 2.0 * u) / (slope * t)) + 1.0).astype(jnp.int32)
    reach = jnp.where(unshifted, reach, n)
    j_lo = jnp.maximum(i - reach, 0)
    j_hi = jnp.minimum(i + reach, n - 1)

    acc_ref[...] = jnp.zeros(acc_ref.shape, F32)

    def scores(mi, qi, rows):
        km_ref = k1_ref if mi == 0 else k2_ref
        s = lax.dot_general(qv_ref[mi, qi], km_ref[rows, :], _NT, preferred_element_type=F32)
        if qi == diag_q:
            s = s - dist_ref[...]
        return s

    def plain_tile(j, qi, width=1):
        rows = pl.ds(pl.multiple_of(j * t, t), width * t)
        va = va_ref[rows, :]
        for mi in range(2):
            p = jnp.exp(scores(mi, qi, rows)).astype(BF16)
            acc_ref[mi] += jnp.dot(p, va, preferred_element_type=F32)

    def online_tile(j, qi, width=1):
        rows = pl.ds(pl.multiple_of(j * t, t), width * t)
        va = va_ref[rows, :]
        for mi in range(2):
            s = scores(mi, qi, rows)
            m_old = m_ref[mi]
            m_new = jnp.maximum(m_old, jnp.max(s, axis=1, keepdims=True))
            p = jnp.exp(s - m_new)
            acc_ref[mi] = (jnp.exp(m_old - m_new) * acc_ref[mi]
                           + jnp.dot(p.astype(BF16), va, preferred_element_type=F32))
            m_ref[mi] = m_new

    def walk(tile, width):
        def run(lo, hi, qi):
            odd = (hi - lo) % width
            for r in range(width - 1):
                @pl.when(r < odd)
                def _single():
                    tile(lo + r, qi)

            def body(c, carry):
                tile(lo + odd + c * width, qi, width)
                return carry
            lax.fori_loop(0, (hi - lo) // width, body, 0)

        run(j_lo, i, before_q)
        tile(i, diag_q)
        run(i + 1, j_hi + 1, after_q)

    def whole_row_step(c, carry):
        for mi in range(2):
            km_ref = k1_ref if mi == 0 else k2_ref
            parts = []
            for w in range(whole):
                j = c * whole + w
                rows = pl.ds(pl.multiple_of(j * t, t), t)
                s = lax.dot_general(qv_ref[mi, jnp.clip(j - i, -1, 1) + 1], km_ref[rows, :], _NT,
                                    preferred_element_type=F32)
                s = s - jnp.where(j == i, 1.0, 0.0) * dist_ref[...]
                parts.append(jnp.dot(jnp.exp(s).astype(BF16), va_ref[rows, :], preferred_element_type=F32))
            acc_ref[mi] += sum(parts)
        return carry

    every_tile = (j_lo == 0) & (j_hi == n - 1)

    @pl.when(unshifted & every_tile)
    def _plain_whole_row():
        lax.fori_loop(0, n // whole, whole_row_step, 0)

    @pl.when(unshifted & jnp.logical_not(every_tile))
    def _plain():
        walk(plain_tile, 2)

    @pl.when(jnp.logical_not(unshifted))
    def _online():
        m_ref[...] = jnp.full(m_ref.shape, NEG, F32)
        walk(online_tile, 1)

    a1 = acc_ref[0]
    a2 = acc_ref[1]
    lf = lam_ref[...]
    lam_full = (jnp.exp(jnp.sum(lf[0:1] * lf[1:2], axis=1, keepdims=True))
                - jnp.exp(jnp.sum(lf[2:3] * lf[3:4], axis=1, keepdims=True)) + lambda_init)
    o = a1[:, :LANES] / a1[:, LANES:] - lam_full * (a2[:, :LANES] / a2[:, LANES:])
    o_ref[0] = (_rms(o, g_ref[...]) * (1.0 - lambda_init)).astype(BF16)


def diff_attn(proj, lam, subln_g, layer_idx, t=512):
    b, s, _ = proj.shape
    nh = N_DIFF_HEADS
    assert (8 % nh) == 0 and t <= 512 and (t & (t - 1)) == 0
    slopes = jnp.asarray([2.0 ** (-8.0 * (h + 1) / nh) for h in range(nh)], F32)
    lambda_init = 0.8 - 0.6 * math.exp(-0.3 * layer_idx)
    grid_spec = pltpu.PrefetchScalarGridSpec(
        num_scalar_prefetch=1,
        grid=(b, nh, s // t),
        in_specs=[pl.BlockSpec((1, t, LANES), lambda bi, h, i, sl: (bi, i, h)),
                  pl.BlockSpec((1, s, LANES), lambda bi, h, i, sl: (bi, 0, nh + h)),
                  pl.BlockSpec((1, s, LANES), lambda bi, h, i, sl: (bi, 0, 2 * nh + h)),
                  pl.BlockSpec((4, HEAD_DIM), lambda bi, h, i, sl: (0, 0)),
                  pl.BlockSpec((1, 2 * HEAD_DIM), lambda bi, h, i, sl: (0, 0))],
        out_specs=pl.BlockSpec((1, t, LANES), lambda bi, h, i, sl: (bi, i, h)),
        scratch_shapes=[pltpu.VMEM((s, LANES), BF16),
                        pltpu.VMEM((s, LANES), BF16),
                        pltpu.VMEM((s, 2 * LANES), BF16),
                        pltpu.VMEM((2, 3, t, LANES), BF16),
                        pltpu.VMEM((t, t), F32),
                        pltpu.VMEM((2, t, 2 * LANES), F32),
                        pltpu.VMEM((2, t, 1), F32),
                        pltpu.SMEM((2,), F32)])
    whole = math.gcd(s // t, 16)
    return pl.pallas_call(
        functools.partial(_diff_attn_kernel, t=t, whole=whole, lambda_init=lambda_init),
        grid_spec=grid_spec,
        out_shape=jax.ShapeDtypeStruct((b, s, nh * 2 * HEAD_DIM), BF16),
        compiler_params=pltpu.CompilerParams(
            dimension_semantics=("arbitrary", "arbitrary", "arbitrary"),
            vmem_limit_bytes=VMEM_LIMIT),
        name="diff_attn")(slopes, proj, proj, proj, lam, subln_g.reshape(1, -1))


def _na_window_start(group, rows):
    return jnp.clip(group * NA_GROUP - NA_KH // 2, 0, rows - NA_WIN)


def _na_bias_tables(rpb, rows):
    n_rho, n_chi = 2 * NA_KH - 1, 2 * NA_KW - 1
    c = np.arange(GRID_W)[:, None]
    kc = np.arange(GRID_W)[None, :]
    cs = np.clip(c - NA_KW // 2, 0, GRID_W - NA_KW)
    col_ok = (kc >= cs) & (kc < cs + NA_KW)
    col_sel = ((kc - c + NA_KW - 1)[..., None] == np.arange(n_chi)) & col_ok[..., None]
    qr = np.arange(NA_GROUP)[:, None]
    kr = np.arange(NA_WIN)[None, :]
    row_sel, row_ok = [], []
    for r0 in (0, NA_GROUP, rows - NA_GROUP):
        ws = min(max(r0 - NA_KH // 2, 0), rows - NA_WIN)
        r = r0 + qr
        rs = np.clip(r - NA_KH // 2, 0, rows - NA_KH)
        ok = (ws + kr >= rs) & (ws + kr < rs + NA_KH)
        row_ok.append(ok)
        row_sel.append(((ws + kr - r + NA_KH - 1)[..., None] == np.arange(n_rho)) & ok[..., None])
    row_sel, row_ok = np.stack(row_sel), np.stack(row_ok)
    by_col = jnp.einsum("hrd,ckd->hrck", rpb.astype(F32), col_sel.astype(np.float32), precision="highest")
    bias = jnp.einsum("hrck,gqnr->ghqcnk", by_col, row_sel.astype(np.float32), precision="highest")
    valid = row_ok[:, None, :, None, :, None] & col_ok[None, None, None, :, None, :]
    shape = (3, rpb.shape[0], NA_GROUP * GRID_W, NA_WIN * GRID_W)
    return jnp.where(valid, bias, NEG).reshape(shape)


def _na_attn_kernel(bmax_ref, q_ref, k_ref, v_ref, t_ref, o_ref, kn_ref, *, rows):
    g = pl.program_id(2)
    nwin = NA_WIN * GRID_W
    tq = q_ref.shape[1]
    n_pairs = q_ref.shape[2] // LANES
    low = lax.broadcasted_iota(jnp.int32, (nwin, LANES), 1) < HEAD_DIM
    low_q = lax.broadcasted_iota(jnp.int32, (tq, LANES), 1) < HEAD_DIM

    @pl.when(g == 0)
    def _key_norms():
        for pi in range(n_pairs):
            lanes = slice(pi * LANES, (pi + 1) * LANES)

            def body(c, norms):
                k = k_ref[0, pl.ds(pl.multiple_of(c * nwin, nwin), nwin), lanes].astype(F32)
                k2 = k * k
                return (jnp.maximum(norms[0], jnp.sum(jnp.where(low, k2, 0.0), axis=1, keepdims=True)),
                        jnp.maximum(norms[1], jnp.sum(jnp.where(low, 0.0, k2), axis=1, keepdims=True)))

            zero = jnp.zeros((nwin, 1), F32)
            n1, n2 = lax.fori_loop(0, k_ref.shape[1] // nwin, body, (zero, zero))
            kn_ref[2 * pi] = jnp.max(n1)
            kn_ref[2 * pi + 1] = jnp.max(n2)

    u2 = jnp.zeros((tq, 1), F32)
    for pi in range(n_pairs):
        qf = q_ref[0, :, pi * LANES:(pi + 1) * LANES].astype(F32)
        q2 = qf * qf
        u2 = jnp.maximum(u2, jnp.maximum(
            jnp.sum(jnp.where(low_q, q2, 0.0), axis=1, keepdims=True) * kn_ref[2 * pi],
            jnp.sum(jnp.where(low_q, 0.0, q2), axis=1, keepdims=True) * kn_ref[2 * pi + 1]))
    unshifted = jnp.max(1.01 * jnp.sqrt(u2)) + bmax_ref[0] <= SCORE_CAP

    start = pl.multiple_of(_na_window_start(g, rows) * GRID_W, 4 * GRID_W)
    win = pl.ds(start, nwin)

    def head_pair(pi, plain):
        lanes = slice(pi * LANES, (pi + 1) * LANES)
        kw = k_ref[0, win, lanes]
        vw = v_ref[0, win, lanes]
        q = q_ref[0, :, lanes]
        zero = jnp.zeros_like(kw)
        out = None
        for hi, keep in enumerate((low, ~low)):
            s = (lax.dot_general(q, jnp.where(keep, kw, zero), _NT, preferred_element_type=F32)
                 + t_ref[0, 2 * pi + hi])
            if plain:
                values = jnp.concatenate([jnp.where(keep, vw, zero), jnp.ones(vw.shape, BF16)], axis=1)
                r = jnp.dot(jnp.exp(s).astype(BF16), values, preferred_element_type=F32)
                o = r[:, :LANES] / r[:, LANES:]
            else:
                p = jnp.exp(s - jnp.max(s, axis=1, keepdims=True))
                l = jnp.sum(p, axis=1, keepdims=True)
                o = jnp.dot(p.astype(BF16), jnp.where(keep, vw, zero), preferred_element_type=F32) / l
            out = o if out is None else out + o
        o_ref[0, :, lanes] = out.astype(BF16)

    @pl.when(unshifted)
    def _plain():
        for pi in range(n_pairs):
            head_pair(pi, True)

    @pl.when(jnp.logical_not(unshifted))
    def _shifted():
        for pi in range(n_pairs):
            head_pair(pi, False)


def na_attn(proj, rpb, col0):
    b, s, _ = proj.shape
    rows = s // GRID_W
    assert rows % NA_WIN == 0 and rows >= 2 * NA_WIN
    groups = rows // NA_GROUP
    pairs = N_NA_HEADS // 2
    cb = col0 // LANES
    tables = _na_bias_tables(rpb, rows)
    tq = NA_GROUP * GRID_W

    def case(g):
        return jnp.where(g == 0, 0, jnp.where(g == groups - 1, 2, 1))

    per_step = 2
    assert pairs % per_step == 0 and cb % per_step == 0
    steps, width, cbw = pairs // per_step, per_step * LANES, cb // per_step
    grid_spec = pltpu.PrefetchScalarGridSpec(
        num_scalar_prefetch=1,
        grid=(steps, b, groups),
        in_specs=[pl.BlockSpec((1, tq, width), lambda p, bi, g, bm: (bi, g, cbw + p)),
                  pl.BlockSpec((1, s, width), lambda p, bi, g, bm: (bi, 0, cbw + steps + p)),
                  pl.BlockSpec((1, s, width), lambda p, bi, g, bm: (bi, 0, cbw + 2 * steps + p)),
                  pl.BlockSpec((1, 2 * per_step, tq, NA_WIN * GRID_W), lambda p, bi, g, bm: (case(g), p, 0, 0))],
        out_specs=pl.BlockSpec((1, tq, width), lambda p, bi, g, bm: (bi, g, p)),
        scratch_shapes=[pltpu.SMEM((2 * per_step,), F32)])
    return pl.pallas_call(
        functools.partial(_na_attn_kernel, rows=rows),
        grid_spec=grid_spec,
        out_shape=jax.ShapeDtypeStruct((b, s, N_NA_HEADS * HEAD_DIM), BF16),
        compiler_params=pltpu.CompilerParams(
            dimension_semantics=("arbitrary", "arbitrary", "arbitrary"),
            vmem_limit_bytes=VMEM_LIMIT),
        name="na_attn")(jnp.max(jnp.abs(rpb)).astype(F32).reshape(1), proj, proj, proj, tables)


def _mixer_specs(x_parts, oa, ob, wa, wb, tm, row_tile=lambda i: i):
    x_specs, first = _row_tile_specs(x_parts, tm)
    x_specs = [pl.BlockSpec(s.block_shape, (lambda i, f=s.index_map: f(row_tile(i)))) for s in x_specs]
    specs = x_specs + [pl.BlockSpec((tm, oa.shape[1]), lambda i: (row_tile(i), 0)),
                       pl.BlockSpec((tm, ob.shape[1]), lambda i: (row_tile(i), 0)),
                       pl.BlockSpec(wa.shape, lambda i: (0, 0)),
                       pl.BlockSpec(wb.shape, lambda i: (0, 0))]
    return specs, first


def _mixer_residual(x_refs, first, a_ref, b_ref, wa_ref, wb_ref):
    return (_row_tile(x_refs, first)
            + jnp.dot(a_ref[...], wa_ref[...], preferred_element_type=F32)
            + jnp.dot(b_ref[...], wb_ref[...], preferred_element_type=F32))


def _out_proj_kernel(*refs, n_x, first):
    refs[-1][...] = _mixer_residual(refs[:n_x], first, *refs[n_x:n_x + 4])


def out_proj(x_parts, oa, ob, wa, wb, tm=512):
    n, d = oa.shape[0], wa.shape[1]
    mixer_specs, first = _mixer_specs(x_parts, oa, ob, wa, wb, tm)
    return pl.pallas_call(
        functools.partial(_out_proj_kernel, n_x=len(x_parts), first=first),
        grid=(n // tm,),
        in_specs=mixer_specs,
        out_specs=pl.BlockSpec((tm, d), lambda i: (i, 0)),
        out_shape=jax.ShapeDtypeStruct((n, d), F32),
        compiler_params=pltpu.CompilerParams(dimension_semantics=("parallel",),
                                             vmem_limit_bytes=VMEM_LIMIT),
        name="out_proj")(*x_parts, oa, ob, wa, wb)


def _swiglu_act(a, b):
    return (a * jax.nn.sigmoid(a) * b).astype(BF16)


def _ffn_dense_kernel(x_ref, g_ref, w1_ref, w3_ref, w2_ref, gf_ref, o_ref, *, chunk, final_norm):
    x = x_ref[...]
    h = _rms(x, g_ref[...]).astype(BF16)
    y = x
    for c in range(w1_ref.shape[1] // chunk):
        cols = slice(c * chunk, (c + 1) * chunk)
        act = _swiglu_act(jnp.dot(h, w1_ref[:, cols], preferred_element_type=F32),
                          jnp.dot(h, w3_ref[:, cols], preferred_element_type=F32))
        y = y + jnp.dot(act, w2_ref[cols, :], preferred_element_type=F32)
    o_ref[...] = _rms(y, gf_ref[...]) if final_norm else y


def ffn_dense(x, g, w1, w3, w2, g_final, final_norm, tm=512):
    n, d = x.shape
    f = w1.shape[1]
    chunk = f // 2 if (f // 2) % LANES == 0 else f
    resident = dict(pipeline_mode=pl.Buffered(1))
    return pl.pallas_call(
        functools.partial(_ffn_dense_kernel, chunk=chunk, final_norm=final_norm),
        grid=(n // tm,),
        in_specs=[pl.BlockSpec((tm, d), lambda i: (i, 0)),
                  pl.BlockSpec((1, d), lambda i: (0, 0)),
                  pl.BlockSpec((d, f), lambda i: (0, 0), **resident),
                  pl.BlockSpec((d, f), lambda i: (0, 0), **resident),
                  pl.BlockSpec((f, d), lambda i: (0, 0), **resident),
                  pl.BlockSpec((1, d), lambda i: (0, 0))],
        out_specs=pl.BlockSpec((tm, d), lambda i: (i, 0)),
        out_shape=jax.ShapeDtypeStruct((n, d), F32),
        compiler_params=pltpu.CompilerParams(dimension_semantics=("parallel",),
                                             vmem_limit_bytes=VMEM_LIMIT),
        name="ffn_dense")(x, g, w1, w3, w2, g_final)


MOE_TT = 512
MOE_CH = 16
MOE_R = 2 * MOE_TT + LANES
MOE_LCH = MOE_R // MOE_CH


def _moe_route_kernel(*refs, n_x, first, n_tiles):
    g_ref, rt_ref, xo_ref, stage_ref, cnt_ref, info_ref = refs[n_x + 4:]
    tt = xo_ref.shape[0]
    ne = rt_ref.shape[0]

    @pl.when(pl.program_id(0) == n_tiles)
    def _zero_tile():
        stage_ref[...] = jnp.zeros(stage_ref.shape, BF16)
        cnt_ref[...] = jnp.zeros(cnt_ref.shape, F32)
        info_ref[...] = jnp.zeros(info_ref.shape, F32)

    @pl.when(pl.program_id(0) < n_tiles)
    def _route():
        x = _mixer_residual(refs[:n_x], first, *refs[n_x:n_x + 4])
        xo_ref[...] = x
        h = _rms(x, g_ref[...]).astype(BF16)
        logits = lax.dot_general(rt_ref[...], h, _NT, preferred_element_type=F32)
        erow = lax.broadcasted_iota(jnp.int32, (ne, tt), 0)
        v1 = jnp.max(logits, axis=0, keepdims=True)
        i1 = jnp.min(jnp.where(logits == v1, erow, ne), axis=0, keepdims=True)
        rest = jnp.where(erow == i1, -jnp.inf, logits)
        v2 = jnp.max(rest, axis=0, keepdims=True)
        i2 = jnp.min(jnp.where(rest == v2, erow, ne), axis=0, keepdims=True)
        e2 = jnp.exp(v2 - v1)
        g1 = 1.0 / (1.0 + e2)
        g2 = e2 / (1.0 + e2)
        member = jnp.where((erow == i1) | (erow == i2), 1.0, 0.0)
        before = (lax.broadcasted_iota(jnp.int32, (tt, tt), 0)
                  < lax.broadcasted_iota(jnp.int32, (tt, tt), 1))
        cum = jnp.dot(member.astype(BF16), jnp.where(before, 1.0, 0.0).astype(BF16),
                      preferred_element_type=F32)
        cnt = jnp.sum(member, axis=1, keepdims=True)
        padded = jnp.floor((cnt + (MOE_CH - 1)) * (1.0 / MOE_CH)) * MOE_CH
        padded_b = jnp.broadcast_to(padded, (ne, tt))
        offset = jnp.zeros((1, tt), F32)
        rank1 = jnp.zeros((1, tt), F32)
        rank2 = jnp.zeros((1, tt), F32)
        for e in range(ne):
            pos = offset + cum[e:e + 1]
            rank1 = rank1 + jnp.where(i1 == e, pos, 0.0)
            rank2 = rank2 + jnp.where(i2 == e, pos, 0.0)
            offset = offset + padded_b[e:e + 1]
        r = lax.broadcasted_iota(jnp.int32, (MOE_R, tt), 0).astype(F32)
        perm = jnp.where((r == rank1) | (r == rank2), 1.0, 0.0).astype(BF16)
        stage_ref[0] = jnp.dot(perm, h, preferred_element_type=F32).astype(BF16)
        cnt_ref[0] = jnp.broadcast_to(padded, (ne, LANES))
        irow = lax.broadcasted_iota(jnp.int32, (8, tt), 0)
        info_ref[0] = jnp.where(irow == 0, rank1, jnp.where(irow == 1, rank2,
                                jnp.where(irow == 2, g1, jnp.where(irow == 3, g2, 0.0))))


def moe_route(x_parts, oa, ob, wa, wb, g, router_t):
    n, d = oa.shape[0], wa.shape[1]
    ne = router_t.shape[0]
    assert ne == 8 and n % MOE_TT == 0
    nt = n // MOE_TT

    def row_tile(i):
        return jnp.minimum(i, nt - 1)

    mixer_specs, first = _mixer_specs(x_parts, oa, ob, wa, wb, MOE_TT, row_tile)
    return pl.pallas_call(
        functools.partial(_moe_route_kernel, n_x=len(x_parts), first=first, n_tiles=nt),
        grid=(nt + 1,),
        in_specs=mixer_specs + [pl.BlockSpec((1, d), lambda i: (0, 0)),
                                pl.BlockSpec((ne, d), lambda i: (0, 0))],
        out_specs=[pl.BlockSpec((MOE_TT, d), lambda i: (row_tile(i), 0)),
                   pl.BlockSpec((1, MOE_R, d), lambda i: (i, 0, 0)),
                   pl.BlockSpec((1, ne, LANES), lambda i: (i, 0, 0)),
                   pl.BlockSpec((1, 8, MOE_TT), lambda i: (i, 0, 0))],
        out_shape=[jax.ShapeDtypeStruct((n, d), F32),
                   jax.ShapeDtypeStruct((nt + 1, MOE_R, d), BF16),
                   jax.ShapeDtypeStruct((nt + 1, ne, LANES), F32),
                   jax.ShapeDtypeStruct((nt + 1, 8, MOE_TT), F32)],
        compiler_params=pltpu.CompilerParams(dimension_semantics=("arbitrary",),
                                             vmem_limit_bytes=VMEM_LIMIT),
        name="moe_route")(*x_parts, oa, ob, wa, wb, g, router_t)


def _moe_tables(padded_counts, tm):
    nt, ne = padded_counts.shape
    tile_ch = tm // MOE_CH
    lc = padded_counts // MOE_CH
    loc = jnp.cumsum(lc, axis=1) - lc
    per_expert = jnp.sum(lc, axis=0)
    per_expert_pad = (per_expert + tile_ch - 1) // tile_ch * tile_ch
    expert_end = jnp.cumsum(per_expert_pad)
    expert_base = expert_end - per_expert_pad
    seg_dst = expert_base[None, :] + jnp.cumsum(lc, axis=0) - lc
    max_chunks = (2 * nt * MOE_TT) // MOE_CH + nt * ne + ne * (tile_ch - 1)
    n_tiles = -(-max_chunks // tile_ch)
    n_active = (expert_end[-1] // tile_ch).astype(jnp.int32)
    tile_expert = jnp.minimum(jnp.sum(jnp.arange(n_tiles)[:, None] >= (expert_end // tile_ch)[None, :], axis=1),
                              ne - 1).astype(jnp.int32)
    start = seg_dst.reshape(-1)
    length = lc.reshape(-1)
    source = (jnp.arange(nt)[:, None] * MOE_LCH + loc).reshape(-1)
    d = jnp.arange(n_tiles * tile_ch)[:, None]
    hit = (d >= start[None, :]) & (d < (start + length)[None, :])
    zero_chunk = nt * MOE_LCH
    ffn_src = jnp.sum(jnp.where(hit, (source - start)[None, :] + d, 0), axis=1)
    ffn_src = jnp.where(jnp.any(hit, axis=1), ffn_src, zero_chunk).astype(jnp.int32)
    l = jnp.arange(MOE_LCH)[None, :, None]
    own = (l >= loc[:, None, :]) & (l < (loc + lc)[:, None, :])
    cmb = jnp.sum(jnp.where(own, (seg_dst - loc)[:, None, :] + l, 0), axis=2).astype(jnp.int32)
    n_valid = jnp.sum(lc, axis=1).astype(jnp.int32)
    return tile_expert, n_active.reshape(1), ffn_src, cmb.reshape(-1), n_valid, n_tiles


def _chunk_copies(table_ref, first, count, src_hbm, dst_ref, sem, start):
    def body(c, carry):
        row = pl.multiple_of(table_ref[first + c] * MOE_CH, MOE_CH)
        copy = pltpu.make_async_copy(src_hbm.at[pl.ds(row, MOE_CH), :],
                                     dst_ref.at[pl.ds(pl.multiple_of(c * MOE_CH, MOE_CH), MOE_CH), :], sem)
        if start:
            copy.start()
        else:
            copy.wait()
        return carry
    lax.fori_loop(0, count, body, 0)


def _moe_ffn_kernel(te_ref, nact_ref, src_ref, stage_hbm, w1_ref, w3_ref, w2_ref, o_ref,
                    xbuf, sem, acc_ref, *, tm):
    i = pl.program_id(0)
    j = pl.program_id(1)
    nf = pl.num_programs(1)
    nch = tm // MOE_CH
    nact = nact_ref[0]
    slot = i % 2

    @pl.when((i == 0) & (j == 0))
    def _prime():
        _chunk_copies(src_ref, 0, nch, stage_hbm, xbuf.at[0], sem.at[0], True)

    @pl.when((i < nact) & (j == 0))
    def _rotate():
        _chunk_copies(src_ref, i * nch, nch, stage_hbm, xbuf.at[slot], sem.at[slot], False)

        @pl.when(i + 1 < nact)
        def _prefetch():
            _chunk_copies(src_ref, (i + 1) * nch, nch, stage_hbm, xbuf.at[1 - slot], sem.at[1 - slot], True)

    @pl.when(i < nact)
    def _compute():
        @pl.when(j == 0)
        def _first():
            acc_ref[...] = jnp.zeros(acc_ref.shape, F32)

        x = xbuf[slot]
        tf = w1_ref.shape[2]
        sub = 2 * LANES if tf % (2 * LANES) == 0 else tf
        for c in range(tf // sub):
            cols = slice(c * sub, (c + 1) * sub)
            act = _swiglu_act(jnp.dot(x, w1_ref[0, :, cols], preferred_element_type=F32),
                              jnp.dot(x, w3_ref[0, :, cols], preferred_element_type=F32))
            acc_ref[...] += jnp.dot(act, w2_ref[0, cols, :], preferred_element_type=F32)

        @pl.when(j == nf - 1)
        def _store():
            o_ref[...] = acc_ref[...].astype(BF16)

    @pl.when((i >= nact) & (j == nf - 1))
    def _unused_tile():
        o_ref[...] = jnp.zeros(o_ref.shape, BF16)


def moe_ffn(stage, tile_expert, n_active, ffn_src, w1, w3, w2, n_tiles, tm):
    d = stage.shape[-1]
    f = w1.shape[2]
    tf = next((c for c in (14 * LANES, 4 * LANES) if f % c == 0), f)
    nf = f // tf

    def live(i, na):
        return jnp.minimum(i, na[0] - 1)

    def fcol(i, j, na):
        return jnp.where(i < na[0], j, nf - 1)

    grid_spec = pltpu.PrefetchScalarGridSpec(
        num_scalar_prefetch=3,
        grid=(n_tiles, nf),
        in_specs=[pl.BlockSpec(memory_space=pl.ANY),
                  pl.BlockSpec((1, d, tf), lambda i, j, te, na, src: (te[live(i, na)], 0, fcol(i, j, na))),
                  pl.BlockSpec((1, d, tf), lambda i, j, te, na, src: (te[live(i, na)], 0, fcol(i, j, na))),
                  pl.BlockSpec((1, tf, d), lambda i, j, te, na, src: (te[live(i, na)], fcol(i, j, na), 0))],
        out_specs=pl.BlockSpec((tm, d), lambda i, j, te, na, src: (i, 0)),
        scratch_shapes=[pltpu.VMEM((2, tm, d), BF16),
                        pltpu.SemaphoreType.DMA((2,)),
                        pltpu.VMEM((tm, d), F32)])
    return pl.pallas_call(
        functools.partial(_moe_ffn_kernel, tm=tm),
        grid_spec=grid_spec,
        out_shape=jax.ShapeDtypeStruct((n_tiles * tm, d), BF16),
        compiler_params=pltpu.CompilerParams(dimension_semantics=("arbitrary", "arbitrary"),
                                             vmem_limit_bytes=VMEM_LIMIT),
        name="moe_ffn")(tile_expert, n_active, ffn_src, stage.reshape(-1, d), w1, w3, w2)


def _moe_combine_kernel(cmb_ref, nval_ref, x_ref, info_ref, ys_hbm, gf_ref, *refs, first, final_norm):
    o_refs, (ybuf, sem) = refs[:-2], refs[-2:]
    t = pl.program_id(0)
    nt = pl.num_programs(0)
    slot = t % 2

    def fetch(tile, slot, start):
        nv = nval_ref[tile]
        _chunk_copies(cmb_ref, tile * MOE_LCH, nv, ys_hbm, ybuf.at[slot], sem.at[slot], start)
        if start:
            def clear(l, carry):
                ybuf[slot, pl.ds(pl.multiple_of(l * MOE_CH, MOE_CH), MOE_CH), :] = jnp.zeros(
                    (MOE_CH, ybuf.shape[2]), BF16)
                return carry
            lax.fori_loop(nv, MOE_LCH, clear, 0)

    @pl.when(t == 0)
    def _prime():
        fetch(0, 0, True)

    fetch(t, slot, False)

    @pl.when(t + 1 < nt)
    def _prefetch():
        fetch(t + 1, 1 - slot, True)

    info = info_ref[...]
    tt = info.shape[0]
    col = lax.broadcasted_iota(jnp.int32, (tt, MOE_R), 1).astype(F32)
    pick = jnp.concatenate([jnp.where(col == info[:, 0:1], 1.0, 0.0).astype(BF16),
                            jnp.where(col == info[:, 1:2], 1.0, 0.0).astype(BF16)], axis=0)
    y = jnp.dot(pick, ybuf[slot], preferred_element_type=F32)
    out = x_ref[...] + info[:, 2:3] * y[:tt] + info[:, 3:4] * y[tt:]
    out = _rms(out, gf_ref[...]) if final_norm else out
    if len(o_refs) == 1:
        o_refs[0][...] = out
    else:
        @pl.when(t < first)
        def _head():
            o_refs[0][...] = out

        @pl.when(t >= first)
        def _tail():
            o_refs[1][...] = out


def moe_combine(x, info_cols, ys, cmb, n_valid, g_final, final_norm, row_split):
    n, d = x.shape
    nt = n // MOE_TT
    assert sum(row_split) == n and all(r % MOE_TT == 0 for r in row_split) and len(row_split) <= 2
    first = row_split[0] // MOE_TT
    if len(row_split) == 1:
        out_specs = [pl.BlockSpec((MOE_TT, d), lambda t, c, v: (t, 0))]
    else:
        out_specs = [pl.BlockSpec((MOE_TT, d), lambda t, c, v: (jnp.minimum(t, first - 1), 0)),
                     pl.BlockSpec((MOE_TT, d), lambda t, c, v: (jnp.maximum(t - first, 0), 0))]
    grid_spec = pltpu.PrefetchScalarGridSpec(
        num_scalar_prefetch=2,
        grid=(nt,),
        in_specs=[pl.BlockSpec((MOE_TT, d), lambda t, c, v: (t, 0)),
                  pl.BlockSpec((MOE_TT, 8), lambda t, c, v: (t, 0)),
                  pl.BlockSpec(memory_space=pl.ANY),
                  pl.BlockSpec((1, d), lambda t, c, v: (0, 0))],
        out_specs=out_specs,
        scratch_shapes=[pltpu.VMEM((2, MOE_R, d), BF16),
                        pltpu.SemaphoreType.DMA((2,))])
    return pl.pallas_call(
        functools.partial(_moe_combine_kernel, first=first, final_norm=final_norm),
        grid_spec=grid_spec,
        out_shape=[jax.ShapeDtypeStruct((r, d), F32) for r in row_split],
        compiler_params=pltpu.CompilerParams(dimension_semantics=("arbitrary",),
                                             vmem_limit_bytes=VMEM_LIMIT),
        name="moe_combine")(cmb, n_valid, x, info_cols, ys, g_final)


def moe_top2(x_parts, oa, ob, wa, wb, g, router, w1, w3, w2, g_final, final_norm, row_split, tm=1024):
    n, d = oa.shape[0], wa.shape[1]
    nt = n // MOE_TT
    x, stage, counts, info = moe_route(x_parts, oa, ob, wa, wb, g, router.T.astype(BF16))
    padded_counts = counts[:nt, :, 0].astype(jnp.int32)
    tile_expert, n_active, ffn_src, cmb, n_valid, n_tiles = _moe_tables(padded_counts, tm)
    ys = moe_ffn(stage, tile_expert, n_active, ffn_src, w1, w3, w2, n_tiles, tm)
    info_cols = info[:nt].transpose(0, 2, 1).reshape(n, 8)
    return moe_combine(x, info_cols, ys, cmb, n_valid, g_final, final_norm, row_split)


def _trunk(x_parts, w_in, w_out, norm_mix, norm_ffn, lambda_qk, subln_g, na_rpb,
           ffn_w1, ffn_w3, ffn_w2, moe_router, moe_w1, moe_w3, moe_w2, norm_final):
    s, d = x_parts[0].shape[1:]
    rows = tuple(p.shape[0] * s for p in x_parts)
    b, n = sum(p.shape[0] for p in x_parts), sum(rows)
    depth = w_in.shape[0]
    diff_w = N_DIFF_HEADS * 2 * HEAD_DIM
    na_w = N_NA_HEADS * HEAD_DIM
    scale = HEAD_DIM ** -0.5
    col_scale = jnp.ones((w_in.shape[2],), F32)
    col_scale = col_scale.at[:diff_w].set(scale).at[3 * diff_w:3 * diff_w + na_w].set(scale)
    xs = [p.reshape(-1, d) for p in x_parts]
    for l in range(depth):
        proj = norm_proj(xs, norm_mix[l].reshape(1, d), (w_in[l] * col_scale).astype(BF16))
        proj = proj.reshape(b, s, -1)
        oa = diff_attn(proj, lambda_qk[l], subln_g[l], l).reshape(n, diff_w)
        ob = na_attn(proj, na_rpb[l], 3 * diff_w).reshape(n, na_w)
        wo = w_out[l].astype(BF16)
        mixed = (xs, oa, ob, wo[:diff_w], wo[diff_w:])
        last = l == depth - 1
        i = l // 2
        if l % 2 == 0:
            xs = [ffn_dense(out_proj(*mixed), norm_ffn[l].reshape(1, d), ffn_w1[i].astype(BF16),
                            ffn_w3[i].astype(BF16), ffn_w2[i].astype(BF16),
                            norm_final.reshape(1, d), final_norm=last)]
        else:
            xs = moe_top2(*mixed, norm_ffn[l].reshape(1, d), moe_router[i], moe_w1[i].astype(BF16),
                          moe_w3[i].astype(BF16), moe_w2[i].astype(BF16),
                          norm_final.reshape(1, d), final_norm=last,
                          row_split=rows if last else (n,))
    if len(xs) != len(rows):
        bounds = np.cumsum((0,) + rows)
        xs = [xs[0][lo:hi] for lo, hi in zip(bounds[:-1], bounds[1:])]
    return tuple(y.reshape(-1, s, d) for y in xs)


def kernel(x_prompt, x_sample, w_in, w_out, norm_mix, norm_ffn, lambda_qk, subln_g, na_rpb,
           ffn_w1, ffn_w3, ffn_w2, moe_router, moe_w1, moe_w3, moe_w2, norm_final):
    assert x_prompt.shape[1:] == x_sample.shape[1:]
    return _trunk([x_prompt, x_sample], w_in, w_out, norm_mix, norm_ffn, lambda_qk, subln_g, na_rpb,
                  ffn_w1, ffn_w3, ffn_w2, moe_router, moe_w1, moe_w3, moe_w2, norm_final)
```

```python
import functools
import math

import numpy as np
import jax
import jax.numpy as jnp
from jax import lax
from jax.experimental import pallas as pl
from jax.experimental.pallas import tpu as pltpu

F32 = jnp.float32
BF16 = jnp.bfloat16

RMS_EPS = 1e-5
HEAD_DIM = 64
N_DIFF_HEADS = 4
N_NA_HEADS = 8
GRID_W = 64
NA_KH = 8
NA_KW = 16
NA_GROUP = 8
NA_WIN = 16
TOP_K = 2
LANES = 128
NEG = -1e30
VMEM_LIMIT = 56 * 1024 * 1024

_NT = (((1,), (1,)), ((), ()))


def _rms(x, g):
    return x * lax.rsqrt(jnp.mean(x * x, axis=-1, keepdims=True) + RMS_EPS) * g


def _row_tile_specs(parts, tm):
    first = parts[0].shape[0] // tm
    d = parts[0].shape[1]
    specs = [pl.BlockSpec((tm, d), lambda i: (jnp.minimum(i, first - 1), 0))]
    if len(parts) == 2:
        assert parts[0].shape[0] % tm == 0 and parts[1].shape[0] % tm == 0
        specs.append(pl.BlockSpec((tm, d), lambda i: (jnp.maximum(i - first, 0), 0)))
    return specs, first


def _row_tile(x_refs, first):
    if len(x_refs) == 1:
        return x_refs[0][...]
    return jnp.where(pl.program_id(0) < first, x_refs[0][...], x_refs[1][...])


def _norm_proj_kernel(*refs, n_x, first, chunk):
    g_ref, w_ref, o_ref = refs[n_x:]
    h = _rms(_row_tile(refs[:n_x], first), g_ref[...]).astype(BF16)
    for c in range(o_ref.shape[1] // chunk):
        cols = slice(c * chunk, (c + 1) * chunk)
        o_ref[:, cols] = jnp.dot(h, w_ref[:, cols], preferred_element_type=F32).astype(BF16)


def norm_proj(x_parts, g, w, tm=512):
    n = sum(p.shape[0] for p in x_parts)
    d, c = w.shape
    x_specs, first = _row_tile_specs(x_parts, tm)
    return pl.pallas_call(
        functools.partial(_norm_proj_kernel, n_x=len(x_parts), first=first, chunk=1024),
        grid=(n // tm,),
        in_specs=x_specs + [pl.BlockSpec((1, d), lambda i: (0, 0)),
                            pl.BlockSpec((d, c), lambda i: (0, 0))],
        out_specs=pl.BlockSpec((tm, c), lambda i: (i, 0)),
        out_shape=jax.ShapeDtypeStruct((n, c), BF16),
        compiler_params=pltpu.CompilerParams(dimension_semantics=("parallel",),
                                             vmem_limit_bytes=VMEM_LIMIT),
        name="norm_proj")(*x_parts, g, w)


SCORE_CAP = 40.0
EXP_ZERO = 88.0


def _diff_attn_kernel(slopes_ref, q_ref, k_ref, v_ref, lam_ref, g_ref, o_ref,
                      k1_ref, k2_ref, va_ref, qv_ref, dist_ref, acc_ref, m_ref, kn_ref,
                      *, t, whole, lambda_init):
    h = pl.program_id(1)
    i = pl.program_id(2)
    n = k_ref.shape[1] // t
    slope = slopes_ref[h]
    lane = lax.broadcasted_iota(jnp.int32, (t, LANES), 1)
    low = lane < HEAD_DIM
    centred = (lax.broadcasted_iota(jnp.int32, (t, LANES), 0) - t // 2).astype(F32)

    def extras(base, a, b, c, d):
        return jnp.where(lane == base, a, jnp.where(lane == base + 1, b,
                         jnp.where(lane == base + 2, c, jnp.where(lane == base + 3, d, 0.0))))

    @pl.when(i == 0)
    def _prepare_keys():
        def body(j, norms):
            rows = pl.ds(pl.multiple_of(j * t, t), t)
            k = k_ref[0, rows, :].astype(F32)
            base = slope * jnp.asarray(j * t).astype(F32)
            k1_ref[rows, :] = jnp.where(low, k, extras(HEAD_DIM, -slope, slope * centred, 1.0, base)).astype(BF16)
            k2_ref[rows, :] = jnp.where(low, extras(0, -slope, slope * centred, 1.0, base), k).astype(BF16)
            va_ref[rows, :LANES] = v_ref[0, rows, :]
            va_ref[rows, LANES:] = jnp.ones((t, LANES), BF16)
            k2 = k * k
            n1 = jnp.sum(jnp.where(low, k2, 0.0), axis=1, keepdims=True)
            n2 = jnp.sum(jnp.where(low, 0.0, k2), axis=1, keepdims=True)
            return jnp.maximum(norms[0], n1), jnp.maximum(norms[1], n2)

        zero = jnp.zeros((t, 1), F32)
        n1, n2 = lax.fori_loop(0, n, body, (zero, zero))
        kn_ref[0] = jnp.max(n1)
        kn_ref[1] = jnp.max(n2)
        dist_ref[...] = slope * jnp.abs(lax.broadcasted_iota(jnp.int32, (t, t), 0)
                                        - lax.broadcasted_iota(jnp.int32, (t, t), 1)).astype(F32)

    q = q_ref[0].astype(F32)
    qbase = -slope * jnp.asarray(i * t).astype(F32)
    for si, sign in enumerate((1.0, 0.0, -1.0)):
        qv_ref[0, si] = jnp.where(low, q, sign * extras(HEAD_DIM, centred, 1.0, qbase, 1.0)).astype(BF16)
        qv_ref[1, si] = jnp.where(low, sign * extras(0, centred, 1.0, qbase, 1.0), q).astype(BF16)
    before_q, diag_q, after_q = 0, 1, 2

    q2 = q * q
    u2 = jnp.maximum(jnp.sum(jnp.where(low, q2, 0.0), axis=1, keepdims=True) * kn_ref[0],
                     jnp.sum(jnp.where(low, 0.0, q2), axis=1, keepdims=True) * kn_ref[1])
    u = 1.01 * jnp.sqrt(jnp.max(u2, axis=0, keepdims=True))
    reach = jnp.floor((EXP_ZERO + 2.0 * u) / (slope * t)) + 1.0
    far = 1e6
    code = jnp.max(jnp.where(u <= SCORE_CAP, reach, far))
    unshifted = code < far
    reach = jnp.where(unshifted, code.astype(jnp.int32), n)
    j_lo = jnp.maximum(i - reach, 0)
    j_hi = jnp.minimum(i + reach, n - 1)

    acc_ref[...] = jnp.zeros(acc_ref.shape, F32)

    def scores(mi, qi, rows):
        km_ref = k1_ref if mi == 0 else k2_ref
        s = lax.dot_general(qv_ref[mi, qi], km_ref[rows, :], _NT, preferred_element_type=F32)
        if qi == diag_q:
            s = s - dist_ref[...]
        return s

    def plain_tile(j, qi, width=1):
        rows = pl.ds(pl.multiple_of(j * t, t), width * t)
        va = va_ref[rows, :]
        for mi in range(2):
            p = jnp.exp(scores(mi, qi, rows)).astype(BF16)
            acc_ref[mi] += jnp.dot(p, va, preferred_element_type=F32)

    def online_tile(j, qi, width=1):
        rows = pl.ds(pl.multiple_of(j * t, t), width * t)
        va = va_ref[rows, :]
        for mi in range(2):
            s = scores(mi, qi, rows)
            m_old = m_ref[mi]
            m_new = jnp.maximum(m_old, jnp.max(s, axis=1, keepdims=True))
            p = jnp.exp(s - m_new)
            acc_ref[mi] = (jnp.exp(m_old - m_new) * acc_ref[mi]
                           + jnp.dot(p.astype(BF16), va, preferred_element_type=F32))
            m_ref[mi] = m_new

    def walk(tile, width):
        def run(lo, hi, qi):
            odd = (hi - lo) % width
            for r in range(width - 1):
                @pl.when(r < odd)
                def _single():
                    tile(lo + r, qi)

            def body(c, carry):
                tile(lo + odd + c * width, qi, width)
                return carry
            lax.fori_loop(0, (hi - lo) // width, body, 0)

        run(j_lo, i, before_q)
        tile(i, diag_q)
        run(i + 1, j_hi + 1, after_q)

    def whole_row_step(c, carry):
        for mi in range(2):
            km_ref = k1_ref if mi == 0 else k2_ref
            parts = []
            for w in range(whole):
                j = c * whole + w
                rows = pl.ds(pl.multiple_of(j * t, t), t)
                s = lax.dot_general(qv_ref[mi, jnp.clip(j - i, -1, 1) + 1], km_ref[rows, :], _NT,
                                    preferred_element_type=F32)
                s = s - jnp.where(j == i, 1.0, 0.0) * dist_ref[...]
                parts.append(jnp.dot(jnp.exp(s).astype(BF16), va_ref[rows, :], preferred_element_type=F32))
            acc_ref[mi] += sum(parts)
        return carry

    every_tile = (j_lo == 0) & (j_hi == n - 1)

    @pl.when(unshifted & every_tile)
    def _plain_whole_row():
        lax.fori_loop(0, n // whole, whole_row_step, 0)

    @pl.when(unshifted & jnp.logical_not(every_tile))
    def _plain():
        walk(plain_tile, 2)

    @pl.when(jnp.logical_not(unshifted))
    def _online():
        m_ref[...] = jnp.full(m_ref.shape, NEG, F32)
        walk(online_tile, 1)

    a1 = acc_ref[0]
    a2 = acc_ref[1]
    lf = lam_ref[...]
    lam_full = (jnp.exp(jnp.sum(lf[0:1] * lf[1:2], axis=1, keepdims=True))
                - jnp.exp(jnp.sum(lf[2:3] * lf[3:4], axis=1, keepdims=True)) + lambda_init)
    o = a1[:, :LANES] / a1[:, LANES:] - lam_full * (a2[:, :LANES] / a2[:, LANES:])
    o_ref[0] = (_rms(o, g_ref[...]) * (1.0 - lambda_init)).astype(BF16)


def diff_attn(proj, lam, subln_g, layer_idx, t=512):
    b, s, _ = proj.shape
    nh = N_DIFF_HEADS
    assert (8 % nh) == 0 and t <= 512 and (t & (t - 1)) == 0
    slopes = jnp.asarray([2.0 ** (-8.0 * (h + 1) / nh) for h in range(nh)], F32)
    lambda_init = 0.8 - 0.6 * math.exp(-0.3 * layer_idx)
    grid_spec = pltpu.PrefetchScalarGridSpec(
        num_scalar_prefetch=1,
        grid=(b, nh, s // t),
        in_specs=[pl.BlockSpec((1, t, LANES), lambda bi, h, i, sl: (bi, i, h)),
                  pl.BlockSpec((1, s, LANES), lambda bi, h, i, sl: (bi, 0, nh + h)),
                  pl.BlockSpec((1, s, LANES), lambda bi, h, i, sl: (bi, 0, 2 * nh + h)),
                  pl.BlockSpec((4, HEAD_DIM), lambda bi, h, i, sl: (0, 0)),
                  pl.BlockSpec((1, 2 * HEAD_DIM), lambda bi, h, i, sl: (0, 0))],
        out_specs=pl.BlockSpec((1, t, LANES), lambda bi, h, i, sl: (bi, i, h)),
        scratch_shapes=[pltpu.VMEM((s, LANES), BF16),
                        pltpu.VMEM((s, LANES), BF16),
                        pltpu.VMEM((s, 2 * LANES), BF16),
                        pltpu.VMEM((2, 3, t, LANES), BF16),
                        pltpu.VMEM((t, t), F32),
                        pltpu.VMEM((2, t, 2 * LANES), F32),
                        pltpu.VMEM((2, t, 1), F32),
                        pltpu.SMEM((2,), F32)])
    whole = math.gcd(s // t, 16)
    return pl.pallas_call(
        functools.partial(_diff_attn_kernel, t=t, whole=whole, lambda_init=lambda_init),
        grid_spec=grid_spec,
        out_shape=jax.ShapeDtypeStruct((b, s, nh * 2 * HEAD_DIM), BF16),
        compiler_params=pltpu.CompilerParams(
            dimension_semantics=("arbitrary", "arbitrary", "arbitrary"),
            vmem_limit_bytes=VMEM_LIMIT),
        name="diff_attn")(slopes, proj, proj, proj, lam, subln_g.reshape(1, -1))


def _na_window_start(group, rows):
    return jnp.clip(group * NA_GROUP - NA_KH // 2, 0, rows - NA_WIN)


def _na_bias_tables(rpb, rows):
    n_rho, n_chi = 2 * NA_KH - 1, 2 * NA_KW - 1
    c = np.arange(GRID_W)[:, None]
    kc = np.arange(GRID_W)[None, :]
    cs = np.clip(c - NA_KW // 2, 0, GRID_W - NA_KW)
    col_ok = (kc >= cs) & (kc < cs + NA_KW)
    col_sel = ((kc - c + NA_KW - 1)[..., None] == np.arange(n_chi)) & col_ok[..., None]
    qr = np.arange(NA_GROUP)[:, None]
    kr = np.arange(NA_WIN)[None, :]
    row_sel, row_ok = [], []
    for r0 in (0, NA_GROUP, rows - NA_GROUP):
        ws = min(max(r0 - NA_KH // 2, 0), rows - NA_WIN)
        r = r0 + qr
        rs = np.clip(r - NA_KH // 2, 0, rows - NA_KH)
        ok = (ws + kr >= rs) & (ws + kr < rs + NA_KH)
        row_ok.append(ok)
        row_sel.append(((ws + kr - r + NA_KH - 1)[..., None] == np.arange(n_rho)) & ok[..., None])
    row_sel, row_ok = np.stack(row_sel), np.stack(row_ok)
    by_col = jnp.einsum("hrd,ckd->hrck", rpb.astype(F32), col_sel.astype(np.float32), precision="highest")
    bias = jnp.einsum("hrck,gqnr->ghqcnk", by_col, row_sel.astype(np.float32), precision="highest")
    valid = row_ok[:, None, :, None, :, None] & col_ok[None, None, None, :, None, :]
    shape = (3, rpb.shape[0], NA_GROUP * GRID_W, NA_WIN * GRID_W)
    return jnp.where(valid, bias, NEG).reshape(shape)


def _na_attn_kernel(bmax_ref, q_ref, k_ref, v_ref, t_ref, o_ref, kn_ref, *, rows):
    g = pl.program_id(2)
    nwin = NA_WIN * GRID_W
    tq = q_ref.shape[1]
    n_pairs = q_ref.shape[2] // LANES
    low = lax.broadcasted_iota(jnp.int32, (nwin, LANES), 1) < HEAD_DIM
    low_q = lax.broadcasted_iota(jnp.int32, (tq, LANES), 1) < HEAD_DIM

    @pl.when(g == 0)
    def _key_norms():
        for pi in range(n_pairs):
            lanes = slice(pi * LANES, (pi + 1) * LANES)

            def body(c, norms):
                k = k_ref[0, pl.ds(pl.multiple_of(c * nwin, nwin), nwin), lanes].astype(F32)
                k2 = k * k
                return (jnp.maximum(norms[0], jnp.sum(jnp.where(low, k2, 0.0), axis=1, keepdims=True)),
                        jnp.maximum(norms[1], jnp.sum(jnp.where(low, 0.0, k2), axis=1, keepdims=True)))

            zero = jnp.zeros((nwin, 1), F32)
            n1, n2 = lax.fori_loop(0, k_ref.shape[1] // nwin, body, (zero, zero))
            kn_ref[2 * pi] = jnp.max(n1)
            kn_ref[2 * pi + 1] = jnp.max(n2)

    u2 = jnp.zeros((tq, 1), F32)
    for pi in range(n_pairs):
        qf = q_ref[0, :, pi * LANES:(pi + 1) * LANES].astype(F32)
        q2 = qf * qf
        u2 = jnp.maximum(u2, jnp.maximum(
            jnp.sum(jnp.where(low_q, q2, 0.0), axis=1, keepdims=True) * kn_ref[2 * pi],
            jnp.sum(jnp.where(low_q, 0.0, q2), axis=1, keepdims=True) * kn_ref[2 * pi + 1]))
    unshifted = jnp.max(1.01 * jnp.sqrt(jnp.max(u2, axis=0, keepdims=True))) + bmax_ref[0] <= SCORE_CAP

    start = pl.multiple_of(_na_window_start(g, rows) * GRID_W, 4 * GRID_W)
    win = pl.ds(start, nwin)

    def head_pair(pi, plain):
        lanes = slice(pi * LANES, (pi + 1) * LANES)
        kw = k_ref[0, win, lanes]
        vw = v_ref[0, win, lanes]
        q = q_ref[0, :, lanes]
        zero = jnp.zeros_like(kw)
        out = None
        for hi, keep in enumerate((low, ~low)):
            s = (lax.dot_general(q, jnp.where(keep, kw, zero), _NT, preferred_element_type=F32)
                 + t_ref[0, 2 * pi + hi])
            if plain:
                values = jnp.concatenate([jnp.where(keep, vw, zero), jnp.ones(vw.shape, BF16)], axis=1)
                r = jnp.dot(jnp.exp(s).astype(BF16), values, preferred_element_type=F32)
                o = r[:, :LANES] / r[:, LANES:]
            else:
                p = jnp.exp(s - jnp.max(s, axis=1, keepdims=True))
                l = jnp.sum(p, axis=1, keepdims=True)
                o = jnp.dot(p.astype(BF16), jnp.where(keep, vw, zero), preferred_element_type=F32) / l
            out = o if out is None else out + o
        o_ref[0, :, lanes] = out.astype(BF16)

    @pl.when(unshifted)
    def _plain():
        for pi in range(n_pairs):
            head_pair(pi, True)

    @pl.when(jnp.logical_not(unshifted))
    def _shifted():
        for pi in range(n_pairs):
            head_pair(pi, False)


def na_attn(proj, rpb, col0):
    b, s, _ = proj.shape
    rows = s // GRID_W
    assert rows % NA_WIN == 0 and rows >= 2 * NA_WIN
    groups = rows // NA_GROUP
    pairs = N_NA_HEADS // 2
    cb = col0 // LANES
    tables = _na_bias_tables(rpb, rows)
    tq = NA_GROUP * GRID_W

    def case(g):
        return jnp.where(g == 0, 0, jnp.where(g == groups - 1, 2, 1))

    per_step = 2
    assert pairs % per_step == 0 and cb % per_step == 0
    steps, width, cbw = pairs // per_step, per_step * LANES, cb // per_step
    grid_spec = pltpu.PrefetchScalarGridSpec(
        num_scalar_prefetch=1,
        grid=(steps, b, groups),
        in_specs=[pl.BlockSpec((1, tq, width), lambda p, bi, g, bm: (bi, g, cbw + p)),
                  pl.BlockSpec((1, s, width), lambda p, bi, g, bm: (bi, 0, cbw + steps + p)),
                  pl.BlockSpec((1, s, width), lambda p, bi, g, bm: (bi, 0, cbw + 2 * steps + p)),
                  pl.BlockSpec((1, 2 * per_step, tq, NA_WIN * GRID_W), lambda p, bi, g, bm: (case(g), p, 0, 0))],
        out_specs=pl.BlockSpec((1, tq, width), lambda p, bi, g, bm: (bi, g, p)),
        scratch_shapes=[pltpu.SMEM((2 * per_step,), F32)])
    return pl.pallas_call(
        functools.partial(_na_attn_kernel, rows=rows),
        grid_spec=grid_spec,
        out_shape=jax.ShapeDtypeStruct((b, s, N_NA_HEADS * HEAD_DIM), BF16),
        compiler_params=pltpu.CompilerParams(
            dimension_semantics=("arbitrary", "arbitrary", "arbitrary"),
            vmem_limit_bytes=VMEM_LIMIT),
        name="na_attn")(jnp.max(jnp.abs(rpb)).astype(F32).reshape(1), proj, proj, proj, tables)


def _mixer_specs(x_parts, oa, ob, wa, wb, tm, row_tile=lambda i: i):
    x_specs, first = _row_tile_specs(x_parts, tm)
    x_specs = [pl.BlockSpec(s.block_shape, (lambda i, f=s.index_map: f(row_tile(i)))) for s in x_specs]
    specs = x_specs + [pl.BlockSpec((tm, oa.shape[1]), lambda i: (row_tile(i), 0)),
                       pl.BlockSpec((tm, ob.shape[1]), lambda i: (row_tile(i), 0)),
                       pl.BlockSpec(wa.shape, lambda i: (0, 0)),
                       pl.BlockSpec(wb.shape, lambda i: (0, 0))]
    return specs, first


def _mixer_residual(x_refs, first, a_ref, b_ref, wa_ref, wb_ref):
    return (_row_tile(x_refs, first)
            + jnp.dot(a_ref[...], wa_ref[...], preferred_element_type=F32)
            + jnp.dot(b_ref[...], wb_ref[...], preferred_element_type=F32))


def _out_proj_kernel(*refs, n_x, first):
    refs[-1][...] = _mixer_residual(refs[:n_x], first, *refs[n_x:n_x + 4])


def out_proj(x_parts, oa, ob, wa, wb, tm=512):
    n, d = oa.shape[0], wa.shape[1]
    mixer_specs, first = _mixer_specs(x_parts, oa, ob, wa, wb, tm)
    return pl.pallas_call(
        functools.partial(_out_proj_kernel, n_x=len(x_parts), first=first),
        grid=(n // tm,),
        in_specs=mixer_specs,
        out_specs=pl.BlockSpec((tm, d), lambda i: (i, 0)),
        out_shape=jax.ShapeDtypeStruct((n, d), F32),
        compiler_params=pltpu.CompilerParams(dimension_semantics=("parallel",),
                                             vmem_limit_bytes=VMEM_LIMIT),
        name="out_proj")(*x_parts, oa, ob, wa, wb)


def _swiglu_act(a, b):
    return (a * jax.nn.sigmoid(a) * b).astype(BF16)


def _ffn_dense_kernel(x_ref, g_ref, w1_ref, w3_ref, w2_ref, gf_ref, o_ref, *, chunk, final_norm):
    x = x_ref[...]
    h = _rms(x, g_ref[...]).astype(BF16)
    y = x
    for c in range(w1_ref.shape[1] // chunk):
        cols = slice(c * chunk, (c + 1) * chunk)
        act = _swiglu_act(jnp.dot(h, w1_ref[:, cols], preferred_element_type=F32),
                          jnp.dot(h, w3_ref[:, cols], preferred_element_type=F32))
        y = y + jnp.dot(act, w2_ref[cols, :], preferred_element_type=F32)
    o_ref[...] = _rms(y, gf_ref[...]) if final_norm else y


def ffn_dense(x, g, w1, w3, w2, g_final, final_norm, tm=512):
    n, d = x.shape
    f = w1.shape[1]
    chunk = f // 2 if (f // 2) % LANES == 0 else f
    resident = dict(pipeline_mode=pl.Buffered(1))
    return pl.pallas_call(
        functools.partial(_ffn_dense_kernel, chunk=chunk, final_norm=final_norm),
        grid=(n // tm,),
        in_specs=[pl.BlockSpec((tm, d), lambda i: (i, 0)),
                  pl.BlockSpec((1, d), lambda i: (0, 0)),
                  pl.BlockSpec((d, f), lambda i: (0, 0), **resident),
                  pl.BlockSpec((d, f), lambda i: (0, 0), **resident),
                  pl.BlockSpec((f, d), lambda i: (0, 0), **resident),
                  pl.BlockSpec((1, d), lambda i: (0, 0))],
        out_specs=pl.BlockSpec((tm, d), lambda i: (i, 0)),
        out_shape=jax.ShapeDtypeStruct((n, d), F32),
        compiler_params=pltpu.CompilerParams(dimension_semantics=("parallel",),
                                             vmem_limit_bytes=VMEM_LIMIT),
        name="ffn_dense")(x, g, w1, w3, w2, g_final)


MOE_TT = 512
MOE_CH = 16
MOE_R = 2 * MOE_TT + LANES
MOE_LCH = MOE_R // MOE_CH


def _moe_route_kernel(*refs, n_x, first, n_tiles):
    g_ref, rt_ref, xo_ref, stage_ref, cnt_ref, info_ref = refs[n_x + 4:]
    tt = xo_ref.shape[0]
    ne = rt_ref.shape[0]

    @pl.when(pl.program_id(0) == n_tiles)
    def _zero_tile():
        stage_ref[...] = jnp.zeros(stage_ref.shape, BF16)
        cnt_ref[...] = jnp.zeros(cnt_ref.shape, F32)
        info_ref[...] = jnp.zeros(info_ref.shape, F32)

    @pl.when(pl.program_id(0) < n_tiles)
    def _route():
        x = _mixer_residual(refs[:n_x], first, *refs[n_x:n_x + 4])
        xo_ref[...] = x
        h = _rms(x, g_ref[...]).astype(BF16)
        logits = lax.dot_general(rt_ref[...], h, _NT, preferred_element_type=F32)
        erow = lax.broadcasted_iota(jnp.int32, (ne, tt), 0)
        v1 = jnp.max(logits, axis=0, keepdims=True)
        i1 = jnp.min(jnp.where(logits == v1, erow, ne), axis=0, keepdims=True)
        rest = jnp.where(erow == i1, -jnp.inf, logits)
        v2 = jnp.max(rest, axis=0, keepdims=True)
        i2 = jnp.min(jnp.where(rest == v2, erow, ne), axis=0, keepdims=True)
        e2 = jnp.exp(v2 - v1)
        g1 = 1.0 / (1.0 + e2)
        g2 = e2 / (1.0 + e2)
        member = jnp.where((erow == i1) | (erow == i2), 1.0, 0.0)
        before = (lax.broadcasted_iota(jnp.int32, (tt, tt), 0)
                  < lax.broadcasted_iota(jnp.int32, (tt, tt), 1))
        cum = jnp.dot(member.astype(BF16), jnp.where(before, 1.0, 0.0).astype(BF16),
                      preferred_element_type=F32)
        cnt = jnp.sum(member, axis=1, keepdims=True)
        padded = jnp.floor((cnt + (MOE_CH - 1)) * (1.0 / MOE_CH)) * MOE_CH
        padded_b = jnp.broadcast_to(padded, (ne, tt))
        offset = jnp.zeros((1, tt), F32)
        rank1 = jnp.zeros((1, tt), F32)
        rank2 = jnp.zeros((1, tt), F32)
        for e in range(ne):
            pos = offset + cum[e:e + 1]
            rank1 = rank1 + jnp.where(i1 == e, pos, 0.0)
            rank2 = rank2 + jnp.where(i2 == e, pos, 0.0)
            offset = offset + padded_b[e:e + 1]
        r = lax.broadcasted_iota(jnp.int32, (MOE_R, tt), 0).astype(F32)
        perm = jnp.where((r == rank1) | (r == rank2), 1.0, 0.0).astype(BF16)
        stage_ref[0] = jnp.dot(perm, h, preferred_element_type=F32).astype(BF16)
        cnt_ref[0] = jnp.broadcast_to(padded, (ne, LANES))
        irow = lax.broadcasted_iota(jnp.int32, (8, tt), 0)
        info_ref[0] = jnp.where(irow == 0, rank1, jnp.where(irow == 1, rank2,
                                jnp.where(irow == 2, g1, jnp.where(irow == 3, g2, 0.0))))


def moe_route(x_parts, oa, ob, wa, wb, g, router_t):
    n, d = oa.shape[0], wa.shape[1]
    ne = router_t.shape[0]
    assert ne == 8 and n % MOE_TT == 0
    nt = n // MOE_TT

    def row_tile(i):
        return jnp.minimum(i, nt - 1)

    mixer_specs, first = _mixer_specs(x_parts, oa, ob, wa, wb, MOE_TT, row_tile)
    return pl.pallas_call(
        functools.partial(_moe_route_kernel, n_x=len(x_parts), first=first, n_tiles=nt),
        grid=(nt + 1,),
        in_specs=mixer_specs + [pl.BlockSpec((1, d), lambda i: (0, 0)),
                                pl.BlockSpec((ne, d), lambda i: (0, 0))],
        out_specs=[pl.BlockSpec((MOE_TT, d), lambda i: (row_tile(i), 0)),
                   pl.BlockSpec((1, MOE_R, d), lambda i: (i, 0, 0)),
                   pl.BlockSpec((1, ne, LANES), lambda i: (i, 0, 0)),
                   pl.BlockSpec((1, 8, MOE_TT), lambda i: (i, 0, 0))],
        out_shape=[jax.ShapeDtypeStruct((n, d), F32),
                   jax.ShapeDtypeStruct((nt + 1, MOE_R, d), BF16),
                   jax.ShapeDtypeStruct((nt + 1, ne, LANES), F32),
                   jax.ShapeDtypeStruct((nt + 1, 8, MOE_TT), F32)],
        compiler_params=pltpu.CompilerParams(dimension_semantics=("arbitrary",),
                                             vmem_limit_bytes=VMEM_LIMIT),
        name="moe_route")(*x_parts, oa, ob, wa, wb, g, router_t)


def _moe_tables(padded_counts, tm):
    nt, ne = padded_counts.shape
    tile_ch = tm // MOE_CH
    lc = padded_counts // MOE_CH
    loc = jnp.cumsum(lc, axis=1) - lc
    per_expert = jnp.sum(lc, axis=0)
    per_expert_pad = (per_expert + tile_ch - 1) // tile_ch * tile_ch
    expert_end = jnp.cumsum(per_expert_pad)
    expert_base = expert_end - per_expert_pad
    seg_dst = expert_base[None, :] + jnp.cumsum(lc, axis=0) - lc
    max_chunks = (2 * nt * MOE_TT) // MOE_CH + nt * ne + ne * (tile_ch - 1)
    n_tiles = -(-max_chunks // tile_ch)
    n_active = (expert_end[-1] // tile_ch).astype(jnp.int32)
    tile_expert = jnp.minimum(jnp.sum(jnp.arange(n_tiles)[:, None] >= (expert_end // tile_ch)[None, :], axis=1),
                              ne - 1).astype(jnp.int32)
    start = seg_dst.reshape(-1)
    length = lc.reshape(-1)
    source = (jnp.arange(nt)[:, None] * MOE_LCH + loc).reshape(-1)
    d = jnp.arange(n_tiles * tile_ch)[:, None]
    hit = (d >= start[None, :]) & (d < (start + length)[None, :])
    zero_chunk = nt * MOE_LCH
    ffn_src = jnp.sum(jnp.where(hit, (source - start)[None, :] + d, 0), axis=1)
    ffn_src = jnp.where(jnp.any(hit, axis=1), ffn_src, zero_chunk).astype(jnp.int32)
    l = jnp.arange(MOE_LCH)[None, :, None]
    own = (l >= loc[:, None, :]) & (l < (loc + lc)[:, None, :])
    cmb = jnp.sum(jnp.where(own, (seg_dst - loc)[:, None, :] + l, 0), axis=2).astype(jnp.int32)
    n_valid = jnp.sum(lc, axis=1).astype(jnp.int32)
    return tile_expert, n_active.reshape(1), ffn_src, cmb.reshape(-1), n_valid, n_tiles


def _chunk_copies(table_ref, first, count, src_hbm, dst_ref, sem, start):
    def body(c, carry):
        row = pl.multiple_of(table_ref[first + c] * MOE_CH, MOE_CH)
        copy = pltpu.make_async_copy(src_hbm.at[pl.ds(row, MOE_CH), :],
                                     dst_ref.at[pl.ds(pl.multiple_of(c * MOE_CH, MOE_CH), MOE_CH), :], sem)
        if start:
            copy.start()
        else:
            copy.wait()
        return carry
    lax.fori_loop(0, count, body, 0)


def _moe_ffn_kernel(te_ref, nact_ref, src_ref, stage_hbm, w1_ref, w3_ref, w2_ref, o_ref,
                    xbuf, sem, acc_ref, *, tm):
    i = pl.program_id(0)
    j = pl.program_id(1)
    nf = pl.num_programs(1)
    nch = tm // MOE_CH
    nact = nact_ref[0]
    slot = i % 2

    @pl.when((i == 0) & (j == 0))
    def _prime():
        _chunk_copies(src_ref, 0, nch, stage_hbm, xbuf.at[0], sem.at[0], True)

    @pl.when((i < nact) & (j == 0))
    def _rotate():
        _chunk_copies(src_ref, i * nch, nch, stage_hbm, xbuf.at[slot], sem.at[slot], False)

        @pl.when(i + 1 < nact)
        def _prefetch():
            _chunk_copies(src_ref, (i + 1) * nch, nch, stage_hbm, xbuf.at[1 - slot], sem.at[1 - slot], True)

    @pl.when(i < nact)
    def _compute():
        @pl.when(j == 0)
        def _first():
            acc_ref[...] = jnp.zeros(acc_ref.shape, F32)

        x = xbuf[slot]
        tf = w1_ref.shape[2]
        sub = 2 * LANES if tf % (2 * LANES) == 0 else tf
        for c in range(tf // sub):
            cols = slice(c * sub, (c + 1) * sub)
            act = _swiglu_act(jnp.dot(x, w1_ref[0, :, cols], preferred_element_type=F32),
                              jnp.dot(x, w3_ref[0, :, cols], preferred_element_type=F32))
            acc_ref[...] += jnp.dot(act, w2_ref[0, cols, :], preferred_element_type=F32)

        @pl.when(j == nf - 1)
        def _store():
            o_ref[...] = acc_ref[...].astype(BF16)

    @pl.when((i >= nact) & (j == nf - 1))
    def _unused_tile():
        o_ref[...] = jnp.zeros(o_ref.shape, BF16)


def moe_ffn(stage, tile_expert, n_active, ffn_src, w1, w3, w2, n_tiles, tm):
    d = stage.shape[-1]
    f = w1.shape[2]
    tf = next((c for c in (14 * LANES, 4 * LANES) if f % c == 0), f)
    nf = f // tf

    def live(i, na):
        return jnp.minimum(i, na[0] - 1)

    def fcol(i, j, na):
        return jnp.where(i < na[0], j, nf - 1)

    grid_spec = pltpu.PrefetchScalarGridSpec(
        num_scalar_prefetch=3,
        grid=(n_tiles, nf),
        in_specs=[pl.BlockSpec(memory_space=pl.ANY),
                  pl.BlockSpec((1, d, tf), lambda i, j, te, na, src: (te[live(i, na)], 0, fcol(i, j, na))),
                  pl.BlockSpec((1, d, tf), lambda i, j, te, na, src: (te[live(i, na)], 0, fcol(i, j, na))),
                  pl.BlockSpec((1, tf, d), lambda i, j, te, na, src: (te[live(i, na)], fcol(i, j, na), 0))],
        out_specs=pl.BlockSpec((tm, d), lambda i, j, te, na, src: (i, 0)),
        scratch_shapes=[pltpu.VMEM((2, tm, d), BF16),
                        pltpu.SemaphoreType.DMA((2,)),
                        pltpu.VMEM((tm, d), F32)])
    return pl.pallas_call(
        functools.partial(_moe_ffn_kernel, tm=tm),
        grid_spec=grid_spec,
        out_shape=jax.ShapeDtypeStruct((n_tiles * tm, d), BF16),
        compiler_params=pltpu.CompilerParams(dimension_semantics=("arbitrary", "arbitrary"),
                                             vmem_limit_bytes=VMEM_LIMIT),
        name="moe_ffn")(tile_expert, n_active, ffn_src, stage.reshape(-1, d), w1, w3, w2)


def _moe_combine_kernel(cmb_ref, nval_ref, x_ref, info_ref, ys_hbm, gf_ref, *refs, first, final_norm):
    o_refs, (ybuf, sem) = refs[:-2], refs[-2:]
    t = pl.program_id(0)
    nt = pl.num_programs(0)
    slot = t % 2

    def fetch(tile, slot, start):
        nv = nval_ref[tile]
        _chunk_copies(cmb_ref, tile * MOE_LCH, nv, ys_hbm, ybuf.at[slot], sem.at[slot], start)
        if start:
            def clear(l, carry):
                ybuf[slot, pl.ds(pl.multiple_of(l * MOE_CH, MOE_CH), MOE_CH), :] = jnp.zeros(
                    (MOE_CH, ybuf.shape[2]), BF16)
                return carry
            lax.fori_loop(nv, MOE_LCH, clear, 0)

    @pl.when(t == 0)
    def _prime():
        fetch(0, 0, True)

    fetch(t, slot, False)

    @pl.when(t + 1 < nt)
    def _prefetch():
        fetch(t + 1, 1 - slot, True)

    info = info_ref[...]
    tt = info.shape[0]
    col = lax.broadcasted_iota(jnp.int32, (tt, MOE_R), 1).astype(F32)
    pick = jnp.concatenate([jnp.where(col == info[:, 0:1], 1.0, 0.0).astype(BF16),
                            jnp.where(col == info[:, 1:2], 1.0, 0.0).astype(BF16)], axis=0)
    y = jnp.dot(pick, ybuf[slot], preferred_element_type=F32)
    out = x_ref[...] + info[:, 2:3] * y[:tt] + info[:, 3:4] * y[tt:]
    out = _rms(out, gf_ref[...]) if final_norm else out
    if len(o_refs) == 1:
        o_refs[0][...] = out
    else:
        @pl.when(t < first)
        def _head():
            o_refs[0][...] = out

        @pl.when(t >= first)
        def _tail():
            o_refs[1][...] = out


def moe_combine(x, info_cols, ys, cmb, n_valid, g_final, final_norm, row_split):
    n, d = x.shape
    nt = n // MOE_TT
    assert sum(row_split) == n and all(r % MOE_TT == 0 for r in row_split) and len(row_split) <= 2
    first = row_split[0] // MOE_TT
    if len(row_split) == 1:
        out_specs = [pl.BlockSpec((MOE_TT, d), lambda t, c, v: (t, 0))]
    else:
        out_specs = [pl.BlockSpec((MOE_TT, d), lambda t, c, v: (jnp.minimum(t, first - 1), 0)),
                     pl.BlockSpec((MOE_TT, d), lambda t, c, v: (jnp.maximum(t - first, 0), 0))]
    grid_spec = pltpu.PrefetchScalarGridSpec(
        num_scalar_prefetch=2,
        grid=(nt,),
        in_specs=[pl.BlockSpec((MOE_TT, d), lambda t, c, v: (t, 0)),
                  pl.BlockSpec((MOE_TT, 8), lambda t, c, v: (t, 0)),
                  pl.BlockSpec(memory_space=pl.ANY),
                  pl.BlockSpec((1, d), lambda t, c, v: (0, 0))],
        out_specs=out_specs,
        scratch_shapes=[pltpu.VMEM((2, MOE_R, d), BF16),
                        pltpu.SemaphoreType.DMA((2,))])
    return pl.pallas_call(
        functools.partial(_moe_combine_kernel, first=first, final_norm=final_norm),
        grid_spec=grid_spec,
        out_shape=[jax.ShapeDtypeStruct((r, d), F32) for r in row_split],
        compiler_params=pltpu.CompilerParams(dimension_semantics=("arbitrary",),
                                             vmem_limit_bytes=VMEM_LIMIT),
        name="moe_combine")(cmb, n_valid, x, info_cols, ys, g_final)


def moe_top2(x_parts, oa, ob, wa, wb, g, router, w1, w3, w2, g_final, final_norm, row_split, tm=1024):
    n, d = oa.shape[0], wa.shape[1]
    nt = n // MOE_TT
    x, stage, counts, info = moe_route(x_parts, oa, ob, wa, wb, g, router.T.astype(BF16))
    padded_counts = counts[:nt, :, 0].astype(jnp.int32)
    tile_expert, n_active, ffn_src, cmb, n_valid, n_tiles = _moe_tables(padded_counts, tm)
    ys = moe_ffn(stage, tile_expert, n_active, ffn_src, w1, w3, w2, n_tiles, tm)
    info_cols = info[:nt].transpose(0, 2, 1).reshape(n, 8)
    return moe_combine(x, info_cols, ys, cmb, n_valid, g_final, final_norm, row_split)


def _trunk(x_parts, w_in, w_out, norm_mix, norm_ffn, lambda_qk, subln_g, na_rpb,
           ffn_w1, ffn_w3, ffn_w2, moe_router, moe_w1, moe_w3, moe_w2, norm_final):
    s, d = x_parts[0].shape[1:]
    rows = tuple(p.shape[0] * s for p in x_parts)
    b, n = sum(p.shape[0] for p in x_parts), sum(rows)
    depth = w_in.shape[0]
    diff_w = N_DIFF_HEADS * 2 * HEAD_DIM
    na_w = N_NA_HEADS * HEAD_DIM
    scale = HEAD_DIM ** -0.5
    col_scale = jnp.ones((w_in.shape[2],), F32)
    col_scale = col_scale.at[:diff_w].set(scale).at[3 * diff_w:3 * diff_w + na_w].set(scale)
    xs = [p.reshape(-1, d) for p in x_parts]
    for l in range(depth):
        proj = norm_proj(xs, norm_mix[l].reshape(1, d), (w_in[l] * col_scale).astype(BF16))
        proj = proj.reshape(b, s, -1)
        oa = diff_attn(proj, lambda_qk[l], subln_g[l], l).reshape(n, diff_w)
        ob = na_attn(proj, na_rpb[l], 3 * diff_w).reshape(n, na_w)
        wo = w_out[l].astype(BF16)
        mixed = (xs, oa, ob, wo[:diff_w], wo[diff_w:])
        last = l == depth - 1
        i = l // 2
        if l % 2 == 0:
            xs = [ffn_dense(out_proj(*mixed), norm_ffn[l].reshape(1, d), ffn_w1[i].astype(BF16),
                            ffn_w3[i].astype(BF16), ffn_w2[i].astype(BF16),
                            norm_final.reshape(1, d), final_norm=last)]
        else:
            xs = moe_top2(*mixed, norm_ffn[l].reshape(1, d), moe_router[i], moe_w1[i].astype(BF16),
                          moe_w3[i].astype(BF16), moe_w2[i].astype(BF16),
                          norm_final.reshape(1, d), final_norm=last,
                          row_split=rows if last else (n,))
    if len(xs) != len(rows):
        bounds = np.cumsum((0,) + rows)
        xs = [xs[0][lo:hi] for lo, hi in zip(bounds[:-1], bounds[1:])]
    return tuple(y.reshape(-1, s, d) for y in xs)


def kernel(x_prompt, x_sample, w_in, w_out, norm_mix, norm_ffn, lambda_qk, subln_g, na_rpb,
           ffn_w1, ffn_w3, ffn_w2, moe_router, moe_w1, moe_w3, moe_w2, norm_final):
    assert x_prompt.shape[1:] == x_sample.shape[1:]
    return _trunk([x_prompt, x_sample], w_in, w_out, norm_mix, norm_ffn, lambda_qk, subln_g, na_rpb,
                  ffn_w1, ffn_w3, ffn_w2, moe_router, moe_w1, moe_w3, moe_w2, norm_final)
```

```python
import functools
import math

import numpy as np
import jax
import jax.numpy as jnp
from jax import lax
from jax.experimental import pallas as pl
from jax.experimental.pallas import tpu as pltpu

F32 = jnp.float32
BF16 = jnp.bfloat16

RMS_EPS = 1e-5
HEAD_DIM = 64
N_DIFF_HEADS = 4
N_NA_HEADS = 8
GRID_W = 64
NA_KH = 8
NA_KW = 16
NA_GROUP = 8
NA_WIN = 16
TOP_K = 2
LANES = 128
NEG = -1e30
VMEM_LIMIT = 56 * 1024 * 1024

_NT = (((1,), (1,)), ((), ()))


def _rms(x, g):
    return x * lax.rsqrt(jnp.mean(x * x, axis=-1, keepdims=True) + RMS_EPS) * g


def _row_tile_specs(parts, tm):
    first = parts[0].shape[0] // tm
    d = parts[0].shape[1]
    specs = [pl.BlockSpec((tm, d), lambda i: (jnp.minimum(i, first - 1), 0))]
    if len(parts) == 2:
        assert parts[0].shape[0] % tm == 0 and parts[1].shape[0] % tm == 0
        specs.append(pl.BlockSpec((tm, d), lambda i: (jnp.maximum(i - first, 0), 0)))
    return specs, first


def _row_tile(x_refs, first):
    if len(x_refs) == 1:
        return x_refs[0][...]
    return jnp.where(pl.program_id(0) < first, x_refs[0][...], x_refs[1][...])


def _norm_proj_kernel(*refs, n_x, first, chunk):
    g_ref, w_ref, o_ref = refs[n_x:]
    h = _rms(_row_tile(refs[:n_x], first), g_ref[...]).astype(BF16)
    for c in range(o_ref.shape[1] // chunk):
        cols = slice(c * chunk, (c + 1) * chunk)
        o_ref[:, cols] = jnp.dot(h, w_ref[:, cols], preferred_element_type=F32).astype(BF16)


def norm_proj(x_parts, g, w, tm=512):
    n = sum(p.shape[0] for p in x_parts)
    d, c = w.shape
    x_specs, first = _row_tile_specs(x_parts, tm)
    return pl.pallas_call(
        functools.partial(_norm_proj_kernel, n_x=len(x_parts), first=first, chunk=1024),
        grid=(n // tm,),
        in_specs=x_specs + [pl.BlockSpec((1, d), lambda i: (0, 0)),
                            pl.BlockSpec((d, c), lambda i: (0, 0))],
        out_specs=pl.BlockSpec((tm, c), lambda i: (i, 0)),
        out_shape=jax.ShapeDtypeStruct((n, c), BF16),
        compiler_params=pltpu.CompilerParams(dimension_semantics=("parallel",),
                                             vmem_limit_bytes=VMEM_LIMIT),
        name="norm_proj")(*x_parts, g, w)


SCORE_CAP = 40.0
EXP_ZERO = 88.0


def _diff_attn_kernel(slopes_ref, q_ref, k_ref, v_ref, lam_ref, g_ref, o_ref,
                      k1_ref, k2_ref, va_ref, qv_ref, dist_ref, acc_ref, m_ref, kn_ref,
                      *, t, whole, lambda_init):
    h = pl.program_id(1)
    i = pl.program_id(2)
    n = k_ref.shape[1] // t
    slope = slopes_ref[h]
    lane = lax.broadcasted_iota(jnp.int32, (t, LANES), 1)
    low = lane < HEAD_DIM
    centred = (lax.broadcasted_iota(jnp.int32, (t, LANES), 0) - t // 2).astype(F32)

    def extras(base, a, b, c, d):
        return jnp.where(lane == base, a, jnp.where(lane == base + 1, b,
                         jnp.where(lane == base + 2, c, jnp.where(lane == base + 3, d, 0.0))))

    @pl.when(i == 0)
    def _prepare_keys():
        def body(j, norms):
            rows = pl.ds(pl.multiple_of(j * t, t), t)
            k = k_ref[0, rows, :].astype(F32)
            base = slope * jnp.asarray(j * t).astype(F32)
            k1_ref[rows, :] = jnp.where(low, k, extras(HEAD_DIM, -slope, slope * centred, 1.0, base)).astype(BF16)
            k2_ref[rows, :] = jnp.where(low, extras(0, -slope, slope * centred, 1.0, base), k).astype(BF16)
            va_ref[rows, :LANES] = v_ref[0, rows, :]
            va_ref[rows, LANES:] = jnp.ones((t, LANES), BF16)
            return jnp.maximum(norms, jnp.dot(k * k, half_sums, preferred_element_type=F32))

        li = lax.broadcasted_iota(jnp.int32, (LANES, LANES), 0)
        ci = lax.broadcasted_iota(jnp.int32, (LANES, LANES), 1)
        half_sums = jnp.where(((ci == 0) & (li < HEAD_DIM)) | ((ci == 1) & (li >= HEAD_DIM)), 1.0, 0.0)
        norms = lax.fori_loop(0, n, body, jnp.zeros((t, LANES), F32))
        kn_ref[0] = jnp.max(norms[:, 0:1])
        kn_ref[1] = jnp.max(norms[:, 1:2])
        dist_ref[...] = slope * jnp.abs(lax.broadcasted_iota(jnp.int32, (t, t), 0)
                                        - lax.broadcasted_iota(jnp.int32, (t, t), 1)).astype(F32)

    q = q_ref[0].astype(F32)
    qbase = -slope * jnp.asarray(i * t).astype(F32)
    for si, sign in enumerate((1.0, 0.0, -1.0)):
        qv_ref[0, si] = jnp.where(low, q, sign * extras(HEAD_DIM, centred, 1.0, qbase, 1.0)).astype(BF16)
        qv_ref[1, si] = jnp.where(low, sign * extras(0, centred, 1.0, qbase, 1.0), q).astype(BF16)
    before_q, diag_q, after_q = 0, 1, 2

    q2 = q * q
    u2 = jnp.maximum(jnp.sum(jnp.where(low, q2, 0.0), axis=1, keepdims=True) * kn_ref[0],
                     jnp.sum(jnp.where(low, 0.0, q2), axis=1, keepdims=True) * kn_ref[1])
    u = 1.01 * jnp.sqrt(jnp.max(u2, axis=0, keepdims=True))
    reach = jnp.floor((EXP_ZERO + 2.0 * u) / (slope * t)) + 1.0
    far = 1e6
    code = jnp.max(jnp.where(u <= SCORE_CAP, reach, far))
    unshifted = code < far
    reach = jnp.where(unshifted, code.astype(jnp.int32), n)
    j_lo = jnp.maximum(i - reach, 0)
    j_hi = jnp.minimum(i + reach, n - 1)

    acc_ref[...] = jnp.zeros(acc_ref.shape, F32)

    def scores(mi, qi, rows):
        km_ref = k1_ref if mi == 0 else k2_ref
        s = lax.dot_general(qv_ref[mi, qi], km_ref[rows, :], _NT, preferred_element_type=F32)
        if qi == diag_q:
            s = s - dist_ref[...]
        return s

    def plain_tile(j, qi, width=1):
        rows = pl.ds(pl.multiple_of(j * t, t), width * t)
        va = va_ref[rows, :]
        for mi in range(2):
            p = jnp.exp(scores(mi, qi, rows)).astype(BF16)
            acc_ref[mi] += jnp.dot(p, va, preferred_element_type=F32)

    def online_tile(j, qi, width=1):
        rows = pl.ds(pl.multiple_of(j * t, t), width * t)
        va = va_ref[rows, :]
        for mi in range(2):
            s = scores(mi, qi, rows)
            m_old = m_ref[mi]
            m_new = jnp.maximum(m_old, jnp.max(s, axis=1, keepdims=True))
            p = jnp.exp(s - m_new)
            acc_ref[mi] = (jnp.exp(m_old - m_new) * acc_ref[mi]
                           + jnp.dot(p.astype(BF16), va, preferred_element_type=F32))
            m_ref[mi] = m_new

    def walk(tile, width):
        def run(lo, hi, qi):
            odd = (hi - lo) % width
            for r in range(width - 1):
                @pl.when(r < odd)
                def _single():
                    tile(lo + r, qi)

            def body(c, carry):
                tile(lo + odd + c * width, qi, width)
                return carry
            lax.fori_loop(0, (hi - lo) // width, body, 0)

        run(j_lo, i, before_q)
        tile(i, diag_q)
        run(i + 1, j_hi + 1, after_q)

    def whole_row_step(c, carry):
        for mi in range(2):
            km_ref = k1_ref if mi == 0 else k2_ref
            parts = []
            for w in range(whole):
                j = c * whole + w
                rows = pl.ds(pl.multiple_of(j * t, t), t)
                s = lax.dot_general(qv_ref[mi, jnp.clip(j - i, -1, 1) + 1], km_ref[rows, :], _NT,
                                    preferred_element_type=F32)
                s = s - jnp.where(j == i, 1.0, 0.0) * dist_ref[...]
                parts.append(jnp.dot(jnp.exp(s).astype(BF16), va_ref[rows, :], preferred_element_type=F32))
            acc_ref[mi] += sum(parts)
        return carry

    every_tile = (j_lo == 0) & (j_hi == n - 1)

    @pl.when(unshifted & every_tile)
    def _plain_whole_row():
        lax.fori_loop(0, n // whole, whole_row_step, 0)

    @pl.when(unshifted & jnp.logical_not(every_tile))
    def _plain():
        walk(plain_tile, 2)

    @pl.when(jnp.logical_not(unshifted))
    def _online():
        m_ref[...] = jnp.full(m_ref.shape, NEG, F32)
        walk(online_tile, 1)

    a1 = acc_ref[0]
    a2 = acc_ref[1]
    lf = lam_ref[...]
    lam_full = (jnp.exp(jnp.sum(lf[0:1] * lf[1:2], axis=1, keepdims=True))
                - jnp.exp(jnp.sum(lf[2:3] * lf[3:4], axis=1, keepdims=True)) + lambda_init)
    o = a1[:, :LANES] / a1[:, LANES:] - lam_full * (a2[:, :LANES] / a2[:, LANES:])
    o_ref[0] = (_rms(o, g_ref[...]) * (1.0 - lambda_init)).astype(BF16)


def diff_attn(proj, lam, subln_g, layer_idx, t=512):
    b, s, _ = proj.shape
    nh = N_DIFF_HEADS
    assert (8 % nh) == 0 and t <= 512 and (t & (t - 1)) == 0
    slopes = jnp.asarray([2.0 ** (-8.0 * (h + 1) / nh) for h in range(nh)], F32)
    lambda_init = 0.8 - 0.6 * math.exp(-0.3 * layer_idx)
    grid_spec = pltpu.PrefetchScalarGridSpec(
        num_scalar_prefetch=1,
        grid=(b, nh, s // t),
        in_specs=[pl.BlockSpec((1, t, LANES), lambda bi, h, i, sl: (bi, i, h)),
                  pl.BlockSpec((1, s, LANES), lambda bi, h, i, sl: (bi, 0, nh + h)),
                  pl.BlockSpec((1, s, LANES), lambda bi, h, i, sl: (bi, 0, 2 * nh + h)),
                  pl.BlockSpec((4, HEAD_DIM), lambda bi, h, i, sl: (0, 0)),
                  pl.BlockSpec((1, 2 * HEAD_DIM), lambda bi, h, i, sl: (0, 0))],
        out_specs=pl.BlockSpec((1, t, LANES), lambda bi, h, i, sl: (bi, i, h)),
        scratch_shapes=[pltpu.VMEM((s, LANES), BF16),
                        pltpu.VMEM((s, LANES), BF16),
                        pltpu.VMEM((s, 2 * LANES), BF16),
                        pltpu.VMEM((2, 3, t, LANES), BF16),
                        pltpu.VMEM((t, t), F32),
                        pltpu.VMEM((2, t, 2 * LANES), F32),
                        pltpu.VMEM((2, t, 1), F32),
                        pltpu.SMEM((2,), F32)])
    whole = math.gcd(s // t, 16)
    return pl.pallas_call(
        functools.partial(_diff_attn_kernel, t=t, whole=whole, lambda_init=lambda_init),
        grid_spec=grid_spec,
        out_shape=jax.ShapeDtypeStruct((b, s, nh * 2 * HEAD_DIM), BF16),
        compiler_params=pltpu.CompilerParams(
            dimension_semantics=("arbitrary", "arbitrary", "arbitrary"),
            vmem_limit_bytes=VMEM_LIMIT),
        name="diff_attn")(slopes, proj, proj, proj, lam, subln_g.reshape(1, -1))


def _na_window_start(group, rows):
    return jnp.clip(group * NA_GROUP - NA_KH // 2, 0, rows - NA_WIN)


def _na_bias_tables(rpb, rows):
    n_rho, n_chi = 2 * NA_KH - 1, 2 * NA_KW - 1
    c = np.arange(GRID_W)[:, None]
    kc = np.arange(GRID_W)[None, :]
    cs = np.clip(c - NA_KW // 2, 0, GRID_W - NA_KW)
    col_ok = (kc >= cs) & (kc < cs + NA_KW)
    col_sel = ((kc - c + NA_KW - 1)[..., None] == np.arange(n_chi)) & col_ok[..., None]
    qr = np.arange(NA_GROUP)[:, None]
    kr = np.arange(NA_WIN)[None, :]
    row_sel, row_ok = [], []
    for r0 in (0, NA_GROUP, rows - NA_GROUP):
        ws = min(max(r0 - NA_KH // 2, 0), rows - NA_WIN)
        r = r0 + qr
        rs = np.clip(r - NA_KH // 2, 0, rows - NA_KH)
        ok = (ws + kr >= rs) & (ws + kr < rs + NA_KH)
        row_ok.append(ok)
        row_sel.append(((ws + kr - r + NA_KH - 1)[..., None] == np.arange(n_rho)) & ok[..., None])
    row_sel, row_ok = np.stack(row_sel), np.stack(row_ok)
    by_col = jnp.einsum("hrd,ckd->hrck", rpb.astype(F32), col_sel.astype(np.float32), precision="highest")
    bias = jnp.einsum("hrck,gqnr->ghqcnk", by_col, row_sel.astype(np.float32), precision="highest")
    valid = row_ok[:, None, :, None, :, None] & col_ok[None, None, None, :, None, :]
    shape = (3, rpb.shape[0], NA_GROUP * GRID_W, NA_WIN * GRID_W)
    return jnp.where(valid, bias, NEG).reshape(shape)


def _na_attn_kernel(bmax_ref, q_ref, k_ref, v_ref, t_ref, o_ref, kn_ref, *, rows):
    g = pl.program_id(2)
    nwin = NA_WIN * GRID_W
    tq = q_ref.shape[1]
    n_pairs = q_ref.shape[2] // LANES
    low = lax.broadcasted_iota(jnp.int32, (nwin, LANES), 1) < HEAD_DIM
    low_q = lax.broadcasted_iota(jnp.int32, (tq, LANES), 1) < HEAD_DIM

    @pl.when(g == 0)
    def _key_norms():
        for pi in range(n_pairs):
            lanes = slice(pi * LANES, (pi + 1) * LANES)

            def body(c, norms):
                k = k_ref[0, pl.ds(pl.multiple_of(c * nwin, nwin), nwin), lanes].astype(F32)
                k2 = k * k
                return (jnp.maximum(norms[0], jnp.sum(jnp.where(low, k2, 0.0), axis=1, keepdims=True)),
                        jnp.maximum(norms[1], jnp.sum(jnp.where(low, 0.0, k2), axis=1, keepdims=True)))

            zero = jnp.zeros((nwin, 1), F32)
            n1, n2 = lax.fori_loop(0, k_ref.shape[1] // nwin, body, (zero, zero))
            kn_ref[2 * pi] = jnp.max(n1)
            kn_ref[2 * pi + 1] = jnp.max(n2)

    u2 = jnp.zeros((tq, 1), F32)
    for pi in range(n_pairs):
        qf = q_ref[0, :, pi * LANES:(pi + 1) * LANES].astype(F32)
        q2 = qf * qf
        u2 = jnp.maximum(u2, jnp.maximum(
            jnp.sum(jnp.where(low_q, q2, 0.0), axis=1, keepdims=True) * kn_ref[2 * pi],
            jnp.sum(jnp.where(low_q, 0.0, q2), axis=1, keepdims=True) * kn_ref[2 * pi + 1]))
    unshifted = jnp.max(1.01 * jnp.sqrt(jnp.max(u2, axis=0, keepdims=True))) + bmax_ref[0] <= SCORE_CAP

    start = pl.multiple_of(_na_window_start(g, rows) * GRID_W, 4 * GRID_W)
    win = pl.ds(start, nwin)

    def head_pair(pi, plain):
        lanes = slice(pi * LANES, (pi + 1) * LANES)
        kw = k_ref[0, win, lanes]
        vw = v_ref[0, win, lanes]
        q = q_ref[0, :, lanes]
        zero = jnp.zeros_like(kw)
        out = None
        for hi, keep in enumerate((low, ~low)):
            s = (lax.dot_general(q, jnp.where(keep, kw, zero), _NT, preferred_element_type=F32)
                 + t_ref[0, 2 * pi + hi])
            if plain:
                values = jnp.concatenate([jnp.where(keep, vw, zero), jnp.ones(vw.shape, BF16)], axis=1)
                r = jnp.dot(jnp.exp(s).astype(BF16), values, preferred_element_type=F32)
                o = r[:, :LANES] / r[:, LANES:]
            else:
                p = jnp.exp(s - jnp.max(s, axis=1, keepdims=True))
                l = jnp.sum(p, axis=1, keepdims=True)
                o = jnp.dot(p.astype(BF16), jnp.where(keep, vw, zero), preferred_element_type=F32) / l
            out = o if out is None else out + o
        o_ref[0, :, lanes] = out.astype(BF16)

    @pl.when(unshifted)
    def _plain():
        for pi in range(n_pairs):
            head_pair(pi, True)

    @pl.when(jnp.logical_not(unshifted))
    def _shifted():
        for pi in range(n_pairs):
            head_pair(pi, False)


def na_attn(proj, rpb, col0):
    b, s, _ = proj.shape
    rows = s // GRID_W
    assert rows % NA_WIN == 0 and rows >= 2 * NA_WIN
    groups = rows // NA_GROUP
    pairs = N_NA_HEADS // 2
    cb = col0 // LANES
    tables = _na_bias_tables(rpb, rows)
    tq = NA_GROUP * GRID_W

    def case(g):
        return jnp.where(g == 0, 0, jnp.where(g == groups - 1, 2, 1))

    per_step = 2
    assert pairs % per_step == 0 and cb % per_step == 0
    steps, width, cbw = pairs // per_step, per_step * LANES, cb // per_step
    grid_spec = pltpu.PrefetchScalarGridSpec(
        num_scalar_prefetch=1,
        grid=(steps, b, groups),
        in_specs=[pl.BlockSpec((1, tq, width), lambda p, bi, g, bm: (bi, g, cbw + p)),
                  pl.BlockSpec((1, s, width), lambda p, bi, g, bm: (bi, 0, cbw + steps + p)),
                  pl.BlockSpec((1, s, width), lambda p, bi, g, bm: (bi, 0, cbw + 2 * steps + p)),
                  pl.BlockSpec((1, 2 * per_step, tq, NA_WIN * GRID_W), lambda p, bi, g, bm: (case(g), p, 0, 0))],
        out_specs=pl.BlockSpec((1, tq, width), lambda p, bi, g, bm: (bi, g, p)),
        scratch_shapes=[pltpu.SMEM((2 * per_step,), F32)])
    return pl.pallas_call(
        functools.partial(_na_attn_kernel, rows=rows),
        grid_spec=grid_spec,
        out_shape=jax.ShapeDtypeStruct((b, s, N_NA_HEADS * HEAD_DIM), BF16),
        compiler_params=pltpu.CompilerParams(
            dimension_semantics=("arbitrary", "arbitrary", "arbitrary"),
            vmem_limit_bytes=VMEM_LIMIT),
        name="na_attn")(jnp.max(jnp.abs(rpb)).astype(F32).reshape(1), proj, proj, proj, tables)


def _mixer_specs(x_parts, oa, ob, wa, wb, tm, row_tile=lambda i: i):
    x_specs, first = _row_tile_specs(x_parts, tm)
    x_specs = [pl.BlockSpec(s.block_shape, (lambda i, f=s.index_map: f(row_tile(i)))) for s in x_specs]
    specs = x_specs + [pl.BlockSpec((tm, oa.shape[1]), lambda i: (row_tile(i), 0)),
                       pl.BlockSpec((tm, ob.shape[1]), lambda i: (row_tile(i), 0)),
                       pl.BlockSpec(wa.shape, lambda i: (0, 0)),
                       pl.BlockSpec(wb.shape, lambda i: (0, 0))]
    return specs, first


def _mixer_residual(x_refs, first, a_ref, b_ref, wa_ref, wb_ref):
    return (_row_tile(x_refs, first)
            + jnp.dot(a_ref[...], wa_ref[...], preferred_element_type=F32)
            + jnp.dot(b_ref[...], wb_ref[...], preferred_element_type=F32))


def _out_proj_kernel(*refs, n_x, first):
    refs[-1][...] = _mixer_residual(refs[:n_x], first, *refs[n_x:n_x + 4])


def out_proj(x_parts, oa, ob, wa, wb, tm=512):
    n, d = oa.shape[0], wa.shape[1]
    mixer_specs, first = _mixer_specs(x_parts, oa, ob, wa, wb, tm)
    return pl.pallas_call(
        functools.partial(_out_proj_kernel, n_x=len(x_parts), first=first),
        grid=(n // tm,),
        in_specs=mixer_specs,
        out_specs=pl.BlockSpec((tm, d), lambda i: (i, 0)),
        out_shape=jax.ShapeDtypeStruct((n, d), F32),
        compiler_params=pltpu.CompilerParams(dimension_semantics=("parallel",),
                                             vmem_limit_bytes=VMEM_LIMIT),
        name="out_proj")(*x_parts, oa, ob, wa, wb)


def _swiglu_act(a, b):
    return (a * jax.nn.sigmoid(a) * b).astype(BF16)


def _ffn_dense_kernel(x_ref, g_ref, w1_ref, w3_ref, w2_ref, gf_ref, o_ref, *, chunk, final_norm):
    x = x_ref[...]
    h = _rms(x, g_ref[...]).astype(BF16)
    y = x
    for c in range(w1_ref.shape[1] // chunk):
        cols = slice(c * chunk, (c + 1) * chunk)
        act = _swiglu_act(jnp.dot(h, w1_ref[:, cols], preferred_element_type=F32),
                          jnp.dot(h, w3_ref[:, cols], preferred_element_type=F32))
        y = y + jnp.dot(act, w2_ref[cols, :], preferred_element_type=F32)
    o_ref[...] = _rms(y, gf_ref[...]) if final_norm else y


def ffn_dense(x, g, w1, w3, w2, g_final, final_norm, tm=512):
    n, d = x.shape
    f = w1.shape[1]
    chunk = f // 2 if (f // 2) % LANES == 0 else f
    resident = dict(pipeline_mode=pl.Buffered(1))
    return pl.pallas_call(
        functools.partial(_ffn_dense_kernel, chunk=chunk, final_norm=final_norm),
        grid=(n // tm,),
        in_specs=[pl.BlockSpec((tm, d), lambda i: (i, 0)),
                  pl.BlockSpec((1, d), lambda i: (0, 0)),
                  pl.BlockSpec((d, f), lambda i: (0, 0), **resident),
                  pl.BlockSpec((d, f), lambda i: (0, 0), **resident),
                  pl.BlockSpec((f, d), lambda i: (0, 0), **resident),
                  pl.BlockSpec((1, d), lambda i: (0, 0))],
        out_specs=pl.BlockSpec((tm, d), lambda i: (i, 0)),
        out_shape=jax.ShapeDtypeStruct((n, d), F32),
        compiler_params=pltpu.CompilerParams(dimension_semantics=("parallel",),
                                             vmem_limit_bytes=VMEM_LIMIT),
        name="ffn_dense")(x, g, w1, w3, w2, g_final)


MOE_TT = 512
MOE_CH = 16
MOE_R = 2 * MOE_TT + LANES
MOE_LCH = MOE_R // MOE_CH


def _moe_route_kernel(*refs, n_x, first, n_tiles):
    g_ref, rt_ref, xo_ref, stage_ref, cnt_ref, info_ref = refs[n_x + 4:]
    tt = xo_ref.shape[0]
    ne = rt_ref.shape[0]

    @pl.when(pl.program_id(0) == n_tiles)
    def _zero_tile():
        stage_ref[...] = jnp.zeros(stage_ref.shape, BF16)
        cnt_ref[...] = jnp.zeros(cnt_ref.shape, F32)
        info_ref[...] = jnp.zeros(info_ref.shape, F32)

    @pl.when(pl.program_id(0) < n_tiles)
    def _route():
        x = _mixer_residual(refs[:n_x], first, *refs[n_x:n_x + 4])
        xo_ref[...] = x
        h = _rms(x, g_ref[...]).astype(BF16)
        logits = lax.dot_general(rt_ref[...], h, _NT, preferred_element_type=F32)
        erow = lax.broadcasted_iota(jnp.int32, (ne, tt), 0)
        v1 = jnp.max(logits, axis=0, keepdims=True)
        i1 = jnp.min(jnp.where(logits == v1, erow, ne), axis=0, keepdims=True)
        rest = jnp.where(erow == i1, -jnp.inf, logits)
        v2 = jnp.max(rest, axis=0, keepdims=True)
        i2 = jnp.min(jnp.where(rest == v2, erow, ne), axis=0, keepdims=True)
        e2 = jnp.exp(v2 - v1)
        g1 = 1.0 / (1.0 + e2)
        g2 = e2 / (1.0 + e2)
        member = jnp.where((erow == i1) | (erow == i2), 1.0, 0.0)
        before = (lax.broadcasted_iota(jnp.int32, (tt, tt), 0)
                  < lax.broadcasted_iota(jnp.int32, (tt, tt), 1))
        cum = jnp.dot(member.astype(BF16), jnp.where(before, 1.0, 0.0).astype(BF16),
                      preferred_element_type=F32)
        cnt = jnp.sum(member, axis=1, keepdims=True)
        padded = jnp.floor((cnt + (MOE_CH - 1)) * (1.0 / MOE_CH)) * MOE_CH
        padded_b = jnp.broadcast_to(padded, (ne, tt))
        offset = jnp.zeros((1, tt), F32)
        rank1 = jnp.zeros((1, tt), F32)
        rank2 = jnp.zeros((1, tt), F32)
        for e in range(ne):
            pos = offset + cum[e:e + 1]
            rank1 = rank1 + jnp.where(i1 == e, pos, 0.0)
            rank2 = rank2 + jnp.where(i2 == e, pos, 0.0)
            offset = offset + padded_b[e:e + 1]
        r = lax.broadcasted_iota(jnp.int32, (MOE_R, tt), 0).astype(F32)
        perm = jnp.where((r == rank1) | (r == rank2), 1.0, 0.0).astype(BF16)
        stage_ref[0] = jnp.dot(perm, h, preferred_element_type=F32).astype(BF16)
        cnt_ref[0] = jnp.broadcast_to(padded, (ne, LANES))
        irow = lax.broadcasted_iota(jnp.int32, (8, tt), 0)
        info_ref[0] = jnp.where(irow == 0, rank1, jnp.where(irow == 1, rank2,
                                jnp.where(irow == 2, g1, jnp.where(irow == 3, g2, 0.0))))


def moe_route(x_parts, oa, ob, wa, wb, g, router_t):
    n, d = oa.shape[0], wa.shape[1]
    ne = router_t.shape[0]
    assert ne == 8 and n % MOE_TT == 0
    nt = n // MOE_TT

    def row_tile(i):
        return jnp.minimum(i, nt - 1)

    mixer_specs, first = _mixer_specs(x_parts, oa, ob, wa, wb, MOE_TT, row_tile)
    return pl.pallas_call(
        functools.partial(_moe_route_kernel, n_x=len(x_parts), first=first, n_tiles=nt),
        grid=(nt + 1,),
        in_specs=mixer_specs + [pl.BlockSpec((1, d), lambda i: (0, 0)),
                                pl.BlockSpec((ne, d), lambda i: (0, 0))],
        out_specs=[pl.BlockSpec((MOE_TT, d), lambda i: (row_tile(i), 0)),
                   pl.BlockSpec((1, MOE_R, d), lambda i: (i, 0, 0)),
                   pl.BlockSpec((1, ne, LANES), lambda i: (i, 0, 0)),
                   pl.BlockSpec((1, 8, MOE_TT), lambda i: (i, 0, 0))],
        out_shape=[jax.ShapeDtypeStruct((n, d), F32),
                   jax.ShapeDtypeStruct((nt + 1, MOE_R, d), BF16),
                   jax.ShapeDtypeStruct((nt + 1, ne, LANES), F32),
                   jax.ShapeDtypeStruct((nt + 1, 8, MOE_TT), F32)],
        compiler_params=pltpu.CompilerParams(dimension_semantics=("arbitrary",),
                                             vmem_limit_bytes=VMEM_LIMIT),
        name="moe_route")(*x_parts, oa, ob, wa, wb, g, router_t)


def _moe_tables(padded_counts, tm):
    nt, ne = padded_counts.shape
    tile_ch = tm // MOE_CH
    lc = padded_counts // MOE_CH
    loc = jnp.cumsum(lc, axis=1) - lc
    per_expert = jnp.sum(lc, axis=0)
    per_expert_pad = (per_expert + tile_ch - 1) // tile_ch * tile_ch
    expert_end = jnp.cumsum(per_expert_pad)
    expert_base = expert_end - per_expert_pad
    seg_dst = expert_base[None, :] + jnp.cumsum(lc, axis=0) - lc
    max_chunks = (2 * nt * MOE_TT) // MOE_CH + nt * ne + ne * (tile_ch - 1)
    n_tiles = -(-max_chunks // tile_ch)
    n_active = (expert_end[-1] // tile_ch).astype(jnp.int32)
    tile_expert = jnp.minimum(jnp.sum(jnp.arange(n_tiles)[:, None] >= (expert_end // tile_ch)[None, :], axis=1),
                              ne - 1).astype(jnp.int32)
    start = seg_dst.reshape(-1)
    length = lc.reshape(-1)
    source = (jnp.arange(nt)[:, None] * MOE_LCH + loc).reshape(-1)
    d = jnp.arange(n_tiles * tile_ch)[:, None]
    hit = (d >= start[None, :]) & (d < (start + length)[None, :])
    zero_chunk = nt * MOE_LCH
    ffn_src = jnp.sum(jnp.where(hit, (source - start)[None, :] + d, 0), axis=1)
    ffn_src = jnp.where(jnp.any(hit, axis=1), ffn_src, zero_chunk).astype(jnp.int32)
    l = jnp.arange(MOE_LCH)[None, :, None]
    own = (l >= loc[:, None, :]) & (l < (loc + lc)[:, None, :])
    cmb = jnp.sum(jnp.where(own, (seg_dst - loc)[:, None, :] + l, 0), axis=2).astype(jnp.int32)
    n_valid = jnp.sum(lc, axis=1).astype(jnp.int32)
    return tile_expert, n_active.reshape(1), ffn_src, cmb.reshape(-1), n_valid, n_tiles


def _chunk_copies(table_ref, first, count, src_hbm, dst_ref, sem, start):
    def one(c):
        row = pl.multiple_of(table_ref[first + c] * MOE_CH, MOE_CH)
        copy = pltpu.make_async_copy(src_hbm.at[pl.ds(row, MOE_CH), :],
                                     dst_ref.at[pl.ds(pl.multiple_of(c * MOE_CH, MOE_CH), MOE_CH), :], sem)
        if start:
            copy.start()
        else:
            copy.wait()

    group = 8 if isinstance(count, int) and count % 8 == 0 else 1

    def body(g, carry):
        for u in range(group):
            one(g * group + u)
        return carry
    lax.fori_loop(0, count // group, body, 0)


def _moe_ffn_kernel(te_ref, nact_ref, src_ref, stage_hbm, w1_ref, w3_ref, w2_ref, o_ref,
                    xbuf, sem, acc_ref, *, tm):
    i = pl.program_id(0)
    j = pl.program_id(1)
    nf = pl.num_programs(1)
    nch = tm // MOE_CH
    nact = nact_ref[0]
    slot = i % 2

    @pl.when((i == 0) & (j == 0))
    def _prime():
        _chunk_copies(src_ref, 0, nch, stage_hbm, xbuf.at[0], sem.at[0], True)

    @pl.when((i < nact) & (j == 0))
    def _rotate():
        _chunk_copies(src_ref, i * nch, nch, stage_hbm, xbuf.at[slot], sem.at[slot], False)

        @pl.when(i + 1 < nact)
        def _prefetch():
            _chunk_copies(src_ref, (i + 1) * nch, nch, stage_hbm, xbuf.at[1 - slot], sem.at[1 - slot], True)

    @pl.when(i < nact)
    def _compute():
        @pl.when(j == 0)
        def _first():
            acc_ref[...] = jnp.zeros(acc_ref.shape, F32)

        x = xbuf[slot]
        tf = w1_ref.shape[2]
        sub = 2 * LANES if tf % (2 * LANES) == 0 else tf
        for c in range(tf // sub):
            cols = slice(c * sub, (c + 1) * sub)
            act = _swiglu_act(jnp.dot(x, w1_ref[0, :, cols], preferred_element_type=F32),
                              jnp.dot(x, w3_ref[0, :, cols], preferred_element_type=F32))
            acc_ref[...] += jnp.dot(act, w2_ref[0, cols, :], preferred_element_type=F32)

        @pl.when(j == nf - 1)
        def _store():
            o_ref[...] = acc_ref[...].astype(BF16)

    @pl.when((i >= nact) & (j == nf - 1))
    def _unused_tile():
        o_ref[...] = jnp.zeros(o_ref.shape, BF16)


def moe_ffn(stage, tile_expert, n_active, ffn_src, w1, w3, w2, n_tiles, tm):
    d = stage.shape[-1]
    f = w1.shape[2]
    tf = next((c for c in (14 * LANES, 4 * LANES) if f % c == 0), f)
    nf = f // tf

    def live(i, na):
        return jnp.minimum(i, na[0] - 1)

    def fcol(i, j, na):
        return jnp.where(i < na[0], j, nf - 1)

    grid_spec = pltpu.PrefetchScalarGridSpec(
        num_scalar_prefetch=3,
        grid=(n_tiles, nf),
        in_specs=[pl.BlockSpec(memory_space=pl.ANY),
                  pl.BlockSpec((1, d, tf), lambda i, j, te, na, src: (te[live(i, na)], 0, fcol(i, j, na))),
                  pl.BlockSpec((1, d, tf), lambda i, j, te, na, src: (te[live(i, na)], 0, fcol(i, j, na))),
                  pl.BlockSpec((1, tf, d), lambda i, j, te, na, src: (te[live(i, na)], fcol(i, j, na), 0))],
        out_specs=pl.BlockSpec((tm, d), lambda i, j, te, na, src: (i, 0)),
        scratch_shapes=[pltpu.VMEM((2, tm, d), BF16),
                        pltpu.SemaphoreType.DMA((2,)),
                        pltpu.VMEM((tm, d), F32)])
    return pl.pallas_call(
        functools.partial(_moe_ffn_kernel, tm=tm),
        grid_spec=grid_spec,
        out_shape=jax.ShapeDtypeStruct((n_tiles * tm, d), BF16),
        compiler_params=pltpu.CompilerParams(dimension_semantics=("arbitrary", "arbitrary"),
                                             vmem_limit_bytes=VMEM_LIMIT),
        name="moe_ffn")(tile_expert, n_active, ffn_src, stage.reshape(-1, d), w1, w3, w2)


def _moe_combine_kernel(cmb_ref, nval_ref, x_ref, info_ref, ys_hbm, gf_ref, *refs, first, final_norm):
    o_refs, (ybuf, sem) = refs[:-2], refs[-2:]
    t = pl.program_id(0)
    nt = pl.num_programs(0)
    slot = t % 2

    def fetch(tile, slot, start):
        nv = nval_ref[tile]
        _chunk_copies(cmb_ref, tile * MOE_LCH, nv, ys_hbm, ybuf.at[slot], sem.at[slot], start)
        if start:
            def clear(l, carry):
                ybuf[slot, pl.ds(pl.multiple_of(l * MOE_CH, MOE_CH), MOE_CH), :] = jnp.zeros(
                    (MOE_CH, ybuf.shape[2]), BF16)
                return carry
            lax.fori_loop(nv, MOE_LCH, clear, 0)

    @pl.when(t == 0)
    def _prime():
        fetch(0, 0, True)

    fetch(t, slot, False)

    @pl.when(t + 1 < nt)
    def _prefetch():
        fetch(t + 1, 1 - slot, True)

    info = info_ref[...]
    tt = info.shape[0]
    col = lax.broadcasted_iota(jnp.int32, (tt, MOE_R), 1).astype(F32)
    pick = jnp.concatenate([jnp.where(col == info[:, 0:1], 1.0, 0.0).astype(BF16),
                            jnp.where(col == info[:, 1:2], 1.0, 0.0).astype(BF16)], axis=0)
    y = jnp.dot(pick, ybuf[slot], preferred_element_type=F32)
    out = x_ref[...] + info[:, 2:3] * y[:tt] + info[:, 3:4] * y[tt:]
    out = _rms(out, gf_ref[...]) if final_norm else out
    if len(o_refs) == 1:
        o_refs[0][...] = out
    else:
        @pl.when(t < first)
        def _head():
            o_refs[0][...] = out

        @pl.when(t >= first)
        def _tail():
            o_refs[1][...] = out


def moe_combine(x, info_cols, ys, cmb, n_valid, g_final, final_norm, row_split):
    n, d = x.shape
    nt = n // MOE_TT
    assert sum(row_split) == n and all(r % MOE_TT == 0 for r in row_split) and len(row_split) <= 2
    first = row_split[0] // MOE_TT
    if len(row_split) == 1:
        out_specs = [pl.BlockSpec((MOE_TT, d), lambda t, c, v: (t, 0))]
    else:
        out_specs = [pl.BlockSpec((MOE_TT, d), lambda t, c, v: (jnp.minimum(t, first - 1), 0)),
                     pl.BlockSpec((MOE_TT, d), lambda t, c, v: (jnp.maximum(t - first, 0), 0))]
    grid_spec = pltpu.PrefetchScalarGridSpec(
        num_scalar_prefetch=2,
        grid=(nt,),
        in_specs=[pl.BlockSpec((MOE_TT, d), lambda t, c, v: (t, 0)),
                  pl.BlockSpec((MOE_TT, 8), lambda t, c, v: (t, 0)),
                  pl.BlockSpec(memory_space=pl.ANY),
                  pl.BlockSpec((1, d), lambda t, c, v: (0, 0))],
        out_specs=out_specs,
        scratch_shapes=[pltpu.VMEM((2, MOE_R, d), BF16),
                        pltpu.SemaphoreType.DMA((2,))])
    return pl.pallas_call(
        functools.partial(_moe_combine_kernel, first=first, final_norm=final_norm),
        grid_spec=grid_spec,
        out_shape=[jax.ShapeDtypeStruct((r, d), F32) for r in row_split],
        compiler_params=pltpu.CompilerParams(dimension_semantics=("arbitrary",),
                                             vmem_limit_bytes=VMEM_LIMIT),
        name="moe_combine")(cmb, n_valid, x, info_cols, ys, g_final)


def moe_top2(x_parts, oa, ob, wa, wb, g, router, w1, w3, w2, g_final, final_norm, row_split, tm=1024):
    n, d = oa.shape[0], wa.shape[1]
    nt = n // MOE_TT
    x, stage, counts, info = moe_route(x_parts, oa, ob, wa, wb, g, router.T.astype(BF16))
    padded_counts = counts[:nt, :, 0].astype(jnp.int32)
    tile_expert, n_active, ffn_src, cmb, n_valid, n_tiles = _moe_tables(padded_counts, tm)
    ys = moe_ffn(stage, tile_expert, n_active, ffn_src, w1, w3, w2, n_tiles, tm)
    info_cols = info[:nt].transpose(0, 2, 1).reshape(n, 8)
    return moe_combine(x, info_cols, ys, cmb, n_valid, g_final, final_norm, row_split)


def _trunk(x_parts, w_in, w_out, norm_mix, norm_ffn, lambda_qk, subln_g, na_rpb,
           ffn_w1, ffn_w3, ffn_w2, moe_router, moe_w1, moe_w3, moe_w2, norm_final):
    s, d = x_parts[0].shape[1:]
    rows = tuple(p.shape[0] * s for p in x_parts)
    b, n = sum(p.shape[0] for p in x_parts), sum(rows)
    depth = w_in.shape[0]
    diff_w = N_DIFF_HEADS * 2 * HEAD_DIM
    na_w = N_NA_HEADS * HEAD_DIM
    scale = HEAD_DIM ** -0.5
    col_scale = jnp.ones((w_in.shape[2],), F32)
    col_scale = col_scale.at[:diff_w].set(scale).at[3 * diff_w:3 * diff_w + na_w].set(scale)
    xs = [p.reshape(-1, d) for p in x_parts]
    for l in range(depth):
        proj = norm_proj(xs, norm_mix[l].reshape(1, d), (w_in[l] * col_scale).astype(BF16))
        proj = proj.reshape(b, s, -1)
        oa = diff_attn(proj, lambda_qk[l], subln_g[l], l).reshape(n, diff_w)
        ob = na_attn(proj, na_rpb[l], 3 * diff_w).reshape(n, na_w)
        wo = w_out[l].astype(BF16)
        mixed = (xs, oa, ob, wo[:diff_w], wo[diff_w:])
        last = l == depth - 1
        i = l // 2
        if l % 2 == 0:
            xs = [ffn_dense(out_proj(*mixed), norm_ffn[l].reshape(1, d), ffn_w1[i].astype(BF16),
                            ffn_w3[i].astype(BF16), ffn_w2[i].astype(BF16),
                            norm_final.reshape(1, d), final_norm=last)]
        else:
            xs = moe_top2(*mixed, norm_ffn[l].reshape(1, d), moe_router[i], moe_w1[i].astype(BF16),
                          moe_w3[i].astype(BF16), moe_w2[i].astype(BF16),
                          norm_final.reshape(1, d), final_norm=last,
                          row_split=rows if last else (n,))
    if len(xs) != len(rows):
        bounds = np.cumsum((0,) + rows)
        xs = [xs[0][lo:hi] for lo, hi in zip(bounds[:-1], bounds[1:])]
    return tuple(y.reshape(-1, s, d) for y in xs)


def kernel(x_prompt, x_sample, w_in, w_out, norm_mix, norm_ffn, lambda_qk, subln_g, na_rpb,
           ffn_w1, ffn_w3, ffn_w2, moe_router, moe_w1, moe_w3, moe_w2, norm_final):
    assert x_prompt.shape[1:] == x_sample.shape[1:]
    return _trunk([x_prompt, x_sample], w_in, w_out, norm_mix, norm_ffn, lambda_qk, subln_g, na_rpb,
                  ffn_w1, ffn_w3, ffn_w2, moe_router, moe_w1, moe_w3, moe_w2, norm_final)
```

```python
import functools
import math

import numpy as np
import jax
import jax.numpy as jnp
from jax import lax
from jax.experimental import pallas as pl
from jax.experimental.pallas import tpu as pltpu

F32 = jnp.float32
BF16 = jnp.bfloat16

RMS_EPS = 1e-5
HEAD_DIM = 64
N_DIFF_HEADS = 4
N_NA_HEADS = 8
GRID_W = 64
NA_KH = 8
NA_KW = 16
NA_GROUP = 8
NA_WIN = 16
TOP_K = 2
LANES = 128
NEG = -1e30
VMEM_LIMIT = 56 * 1024 * 1024

_NT = (((1,), (1,)), ((), ()))


def _rms(x, g):
    return x * lax.rsqrt(jnp.mean(x * x, axis=-1, keepdims=True) + RMS_EPS) * g


def _row_tile_specs(parts, tm):
    first = parts[0].shape[0] // tm
    d = parts[0].shape[1]
    specs = [pl.BlockSpec((tm, d), lambda i: (jnp.minimum(i, first - 1), 0))]
    if len(parts) == 2:
        assert parts[0].shape[0] % tm == 0 and parts[1].shape[0] % tm == 0
        specs.append(pl.BlockSpec((tm, d), lambda i: (jnp.maximum(i - first, 0), 0)))
    return specs, first


def _row_tile(x_refs, first):
    if len(x_refs) == 1:
        return x_refs[0][...]
    return jnp.where(pl.program_id(0) < first, x_refs[0][...], x_refs[1][...])


def _norm_proj_kernel(*refs, n_x, first, chunk):
    g_ref, w_ref, o_ref = refs[n_x:]
    h = _rms(_row_tile(refs[:n_x], first), g_ref[...]).astype(BF16)
    for c in range(o_ref.shape[1] // chunk):
        cols = slice(c * chunk, (c + 1) * chunk)
        o_ref[:, cols] = jnp.dot(h, w_ref[:, cols], preferred_element_type=F32).astype(BF16)


def norm_proj(x_parts, g, w, tm=512):
    n = sum(p.shape[0] for p in x_parts)
    d, c = w.shape
    x_specs, first = _row_tile_specs(x_parts, tm)
    return pl.pallas_call(
        functools.partial(_norm_proj_kernel, n_x=len(x_parts), first=first, chunk=1024),
        grid=(n // tm,),
        in_specs=x_specs + [pl.BlockSpec((1, d), lambda i: (0, 0)),
                            pl.BlockSpec((d, c), lambda i: (0, 0))],
        out_specs=pl.BlockSpec((tm, c), lambda i: (i, 0)),
        out_shape=jax.ShapeDtypeStruct((n, c), BF16),
        compiler_params=pltpu.CompilerParams(dimension_semantics=("parallel",),
                                             vmem_limit_bytes=VMEM_LIMIT),
        name="norm_proj")(*x_parts, g, w)


SCORE_CAP = 40.0
EXP_ZERO = 88.0


def _diff_attn_kernel(slopes_ref, q_ref, k_ref, v_ref, lam_ref, g_ref, o_ref,
                      k1_ref, k2_ref, va_ref, qv_ref, dist_ref, acc_ref, m_ref, kn_ref,
                      *, t, whole, lambda_init):
    h = pl.program_id(1)
    i = pl.program_id(2)
    n = k_ref.shape[1] // t
    slope = slopes_ref[h]
    lane = lax.broadcasted_iota(jnp.int32, (t, LANES), 1)
    low = lane < HEAD_DIM
    centred = (lax.broadcasted_iota(jnp.int32, (t, LANES), 0) - t // 2).astype(F32)

    def extras(base, a, b, c, d):
        return jnp.where(lane == base, a, jnp.where(lane == base + 1, b,
                         jnp.where(lane == base + 2, c, jnp.where(lane == base + 3, d, 0.0))))

    @pl.when(i == 0)
    def _prepare_keys():
        def body(j, norms):
            rows = pl.ds(pl.multiple_of(j * t, t), t)
            k = k_ref[0, rows, :].astype(F32)
            base = slope * jnp.asarray(j * t).astype(F32)
            k1_ref[rows, :] = jnp.where(low, k, extras(HEAD_DIM, -slope, slope * centred, 1.0, base)).astype(BF16)
            k2_ref[rows, :] = jnp.where(low, extras(0, -slope, slope * centred, 1.0, base), k).astype(BF16)
            va_ref[rows, :LANES] = v_ref[0, rows, :]
            va_ref[rows, LANES:] = jnp.ones((t, LANES), BF16)
            return jnp.maximum(norms, jnp.dot(k * k, half_sums, preferred_element_type=F32))

        li = lax.broadcasted_iota(jnp.int32, (LANES, LANES), 0)
        ci = lax.broadcasted_iota(jnp.int32, (LANES, LANES), 1)
        half_sums = jnp.where(((ci == 0) & (li < HEAD_DIM)) | ((ci == 1) & (li >= HEAD_DIM)), 1.0, 0.0)
        norms = lax.fori_loop(0, n, body, jnp.zeros((t, LANES), F32))
        kn_ref[0] = jnp.max(norms[:, 0:1])
        kn_ref[1] = jnp.max(norms[:, 1:2])
        dist_ref[...] = slope * jnp.abs(lax.broadcasted_iota(jnp.int32, (t, t), 0)
                                        - lax.broadcasted_iota(jnp.int32, (t, t), 1)).astype(F32)

    q = q_ref[0].astype(F32)
    qbase = -slope * jnp.asarray(i * t).astype(F32)
    for si, sign in enumerate((1.0, 0.0, -1.0)):
        qv_ref[0, si] = jnp.where(low, q, sign * extras(HEAD_DIM, centred, 1.0, qbase, 1.0)).astype(BF16)
        qv_ref[1, si] = jnp.where(low, sign * extras(0, centred, 1.0, qbase, 1.0), q).astype(BF16)
    before_q, diag_q, after_q = 0, 1, 2

    q2 = q * q
    u2 = jnp.maximum(jnp.sum(jnp.where(low, q2, 0.0), axis=1, keepdims=True) * kn_ref[0],
                     jnp.sum(jnp.where(low, 0.0, q2), axis=1, keepdims=True) * kn_ref[1])
    u = 1.01 * jnp.sqrt(jnp.max(u2, axis=0, keepdims=True))
    reach = jnp.floor((EXP_ZERO + 2.0 * u) / (slope * t)) + 1.0
    far = 1e6
    code = jnp.max(jnp.where(u <= SCORE_CAP, reach, far))
    unshifted = code < far
    reach = jnp.where(unshifted, code.astype(jnp.int32), n)
    j_lo = jnp.maximum(i - reach, 0)
    j_hi = jnp.minimum(i + reach, n - 1)

    acc_ref[...] = jnp.zeros(acc_ref.shape, F32)

    def scores(mi, qi, rows):
        km_ref = k1_ref if mi == 0 else k2_ref
        s = lax.dot_general(qv_ref[mi, qi], km_ref[rows, :], _NT, preferred_element_type=F32)
        if qi == diag_q:
            s = s - dist_ref[...]
        return s

    def plain_tile(j, qi, width=1):
        rows = pl.ds(pl.multiple_of(j * t, t), width * t)
        va = va_ref[rows, :]
        for mi in range(2):
            p = jnp.exp(scores(mi, qi, rows)).astype(BF16)
            acc_ref[mi] += jnp.dot(p, va, preferred_element_type=F32)

    def online_tile(j, qi, width=1):
        rows = pl.ds(pl.multiple_of(j * t, t), width * t)
        va = va_ref[rows, :]
        for mi in range(2):
            s = scores(mi, qi, rows)
            m_old = m_ref[mi]
            m_new = jnp.maximum(m_old, jnp.max(s, axis=1, keepdims=True))
            p = jnp.exp(s - m_new)
            acc_ref[mi] = (jnp.exp(m_old - m_new) * acc_ref[mi]
                           + jnp.dot(p.astype(BF16), va, preferred_element_type=F32))
            m_ref[mi] = m_new

    def walk(tile, width):
        def run(lo, hi, qi):
            odd = (hi - lo) % width
            for r in range(width - 1):
                @pl.when(r < odd)
                def _single():
                    tile(lo + r, qi)

            def body(c, carry):
                tile(lo + odd + c * width, qi, width)
                return carry
            lax.fori_loop(0, (hi - lo) // width, body, 0)

        run(j_lo, i, before_q)
        tile(i, diag_q)
        run(i + 1, j_hi + 1, after_q)

    def whole_row_step(c, carry):
        for mi in range(2):
            km_ref = k1_ref if mi == 0 else k2_ref
            parts = []
            for w in range(whole):
                j = c * whole + w
                rows = pl.ds(pl.multiple_of(j * t, t), t)
                s = lax.dot_general(qv_ref[mi, jnp.clip(j - i, -1, 1) + 1], km_ref[rows, :], _NT,
                                    preferred_element_type=F32)
                s = s - jnp.where(j == i, 1.0, 0.0) * dist_ref[...]
                parts.append(jnp.dot(jnp.exp(s).astype(BF16), va_ref[rows, :], preferred_element_type=F32))
            acc_ref[mi] += sum(parts)
        return carry

    every_tile = (j_lo == 0) & (j_hi == n - 1)

    @pl.when(unshifted & every_tile)
    def _plain_whole_row():
        lax.fori_loop(0, n // whole, whole_row_step, 0)

    @pl.when(unshifted & jnp.logical_not(every_tile))
    def _plain():
        walk(plain_tile, 2)

    @pl.when(jnp.logical_not(unshifted))
    def _online():
        m_ref[...] = jnp.full(m_ref.shape, NEG, F32)
        walk(online_tile, 1)

    a1 = acc_ref[0]
    a2 = acc_ref[1]
    lf = lam_ref[...]
    lam_full = (jnp.exp(jnp.sum(lf[0:1] * lf[1:2], axis=1, keepdims=True))
                - jnp.exp(jnp.sum(lf[2:3] * lf[3:4], axis=1, keepdims=True)) + lambda_init)
    o = a1[:, :LANES] / a1[:, LANES:] - lam_full * (a2[:, :LANES] / a2[:, LANES:])
    o_ref[0] = (_rms(o, g_ref[...]) * (1.0 - lambda_init)).astype(BF16)


def diff_attn(proj, lam, subln_g, layer_idx, t=512):
    b, s, _ = proj.shape
    nh = N_DIFF_HEADS
    assert (8 % nh) == 0 and t <= 512 and (t & (t - 1)) == 0
    slopes = jnp.asarray([2.0 ** (-8.0 * (h + 1) / nh) for h in range(nh)], F32)
    lambda_init = 0.8 - 0.6 * math.exp(-0.3 * layer_idx)
    grid_spec = pltpu.PrefetchScalarGridSpec(
        num_scalar_prefetch=1,
        grid=(b, nh, s // t),
        in_specs=[pl.BlockSpec((1, t, LANES), lambda bi, h, i, sl: (bi, i, h)),
                  pl.BlockSpec((1, s, LANES), lambda bi, h, i, sl: (bi, 0, nh + h)),
                  pl.BlockSpec((1, s, LANES), lambda bi, h, i, sl: (bi, 0, 2 * nh + h)),
                  pl.BlockSpec((4, HEAD_DIM), lambda bi, h, i, sl: (0, 0)),
                  pl.BlockSpec((1, 2 * HEAD_DIM), lambda bi, h, i, sl: (0, 0))],
        out_specs=pl.BlockSpec((1, t, LANES), lambda bi, h, i, sl: (bi, i, h)),
        scratch_shapes=[pltpu.VMEM((s, LANES), BF16),
                        pltpu.VMEM((s, LANES), BF16),
                        pltpu.VMEM((s, 2 * LANES), BF16),
                        pltpu.VMEM((2, 3, t, LANES), BF16),
                        pltpu.VMEM((t, t), F32),
                        pltpu.VMEM((2, t, 2 * LANES), F32),
                        pltpu.VMEM((2, t, 1), F32),
                        pltpu.SMEM((2,), F32)])
    whole = math.gcd(s // t, 16)
    return pl.pallas_call(
        functools.partial(_diff_attn_kernel, t=t, whole=whole, lambda_init=lambda_init),
        grid_spec=grid_spec,
        out_shape=jax.ShapeDtypeStruct((b, s, nh * 2 * HEAD_DIM), BF16),
        compiler_params=pltpu.CompilerParams(
            dimension_semantics=("arbitrary", "arbitrary", "arbitrary"),
            vmem_limit_bytes=VMEM_LIMIT),
        name="diff_attn")(slopes, proj, proj, proj, lam, subln_g.reshape(1, -1))


def _na_window_start(group, rows):
    return jnp.clip(group * NA_GROUP - NA_KH // 2, 0, rows - NA_WIN)


def _na_bias_tables(rpb, rows):
    n_rho, n_chi = 2 * NA_KH - 1, 2 * NA_KW - 1
    c = np.arange(GRID_W)[:, None]
    kc = np.arange(GRID_W)[None, :]
    cs = np.clip(c - NA_KW // 2, 0, GRID_W - NA_KW)
    col_ok = (kc >= cs) & (kc < cs + NA_KW)
    col_sel = ((kc - c + NA_KW - 1)[..., None] == np.arange(n_chi)) & col_ok[..., None]
    qr = np.arange(NA_GROUP)[:, None]
    kr = np.arange(NA_WIN)[None, :]
    row_sel, row_ok = [], []
    for r0 in (0, NA_GROUP, rows - NA_GROUP):
        ws = min(max(r0 - NA_KH // 2, 0), rows - NA_WIN)
        r = r0 + qr
        rs = np.clip(r - NA_KH // 2, 0, rows - NA_KH)
        ok = (ws + kr >= rs) & (ws + kr < rs + NA_KH)
        row_ok.append(ok)
        row_sel.append(((ws + kr - r + NA_KH - 1)[..., None] == np.arange(n_rho)) & ok[..., None])
    row_sel, row_ok = np.stack(row_sel), np.stack(row_ok)
    by_col = jnp.einsum("hrd,ckd->hrck", rpb.astype(F32), col_sel.astype(np.float32), precision="highest")
    bias = jnp.einsum("hrck,gqnr->ghqcnk", by_col, row_sel.astype(np.float32), precision="highest")
    valid = row_ok[:, None, :, None, :, None] & col_ok[None, None, None, :, None, :]
    shape = (3, rpb.shape[0], NA_GROUP * GRID_W, NA_WIN * GRID_W)
    return jnp.where(valid, bias, NEG).reshape(shape)


def _na_attn_kernel(bmax_ref, q_ref, k_ref, v_ref, t_ref, o_ref, kn_ref, *, rows):
    g = pl.program_id(2)
    nwin = NA_WIN * GRID_W
    tq = q_ref.shape[1]
    n_pairs = q_ref.shape[2] // LANES
    low = lax.broadcasted_iota(jnp.int32, (nwin, LANES), 1) < HEAD_DIM
    low_q = lax.broadcasted_iota(jnp.int32, (tq, LANES), 1) < HEAD_DIM

    @pl.when(g == 0)
    def _key_norms():
        for pi in range(n_pairs):
            lanes = slice(pi * LANES, (pi + 1) * LANES)

            def body(c, norms):
                k = k_ref[0, pl.ds(pl.multiple_of(c * nwin, nwin), nwin), lanes].astype(F32)
                k2 = k * k
                return (jnp.maximum(norms[0], jnp.sum(jnp.where(low, k2, 0.0), axis=1, keepdims=True)),
                        jnp.maximum(norms[1], jnp.sum(jnp.where(low, 0.0, k2), axis=1, keepdims=True)))

            zero = jnp.zeros((nwin, 1), F32)
            n1, n2 = lax.fori_loop(0, k_ref.shape[1] // nwin, body, (zero, zero))
            kn_ref[2 * pi] = jnp.max(n1)
            kn_ref[2 * pi + 1] = jnp.max(n2)

    u2 = jnp.zeros((tq, 1), F32)
    for pi in range(n_pairs):
        qf = q_ref[0, :, pi * LANES:(pi + 1) * LANES].astype(F32)
        q2 = qf * qf
        u2 = jnp.maximum(u2, jnp.maximum(
            jnp.sum(jnp.where(low_q, q2, 0.0), axis=1, keepdims=True) * kn_ref[2 * pi],
            jnp.sum(jnp.where(low_q, 0.0, q2), axis=1, keepdims=True) * kn_ref[2 * pi + 1]))
    unshifted = jnp.max(1.01 * jnp.sqrt(jnp.max(u2, axis=0, keepdims=True))) + bmax_ref[0] <= SCORE_CAP

    start = pl.multiple_of(_na_window_start(g, rows) * GRID_W, 4 * GRID_W)
    win = pl.ds(start, nwin)

    def head_pair(pi, plain):
        lanes = slice(pi * LANES, (pi + 1) * LANES)
        kw = k_ref[0, win, lanes]
        vw = v_ref[0, win, lanes]
        q = q_ref[0, :, lanes]
        zero = jnp.zeros_like(kw)
        out = None
        for hi, keep in enumerate((low, ~low)):
            s = (lax.dot_general(q, jnp.where(keep, kw, zero), _NT, preferred_element_type=F32)
                 + t_ref[0, 2 * pi + hi])
            if plain:
                values = jnp.concatenate([jnp.where(keep, vw, zero), jnp.ones(vw.shape, BF16)], axis=1)
                r = jnp.dot(jnp.exp(s).astype(BF16), values, preferred_element_type=F32)
                o = r[:, :LANES] / r[:, LANES:]
            else:
                p = jnp.exp(s - jnp.max(s, axis=1, keepdims=True))
                l = jnp.sum(p, axis=1, keepdims=True)
                o = jnp.dot(p.astype(BF16), jnp.where(keep, vw, zero), preferred_element_type=F32) / l
            out = o if out is None else out + o
        o_ref[0, :, lanes] = out.astype(BF16)

    @pl.when(unshifted)
    def _plain():
        for pi in range(n_pairs):
            head_pair(pi, True)

    @pl.when(jnp.logical_not(unshifted))
    def _shifted():
        for pi in range(n_pairs):
            head_pair(pi, False)


def na_attn(proj, rpb, col0):
    b, s, _ = proj.shape
    rows = s // GRID_W
    assert rows % NA_WIN == 0 and rows >= 2 * NA_WIN
    groups = rows // NA_GROUP
    pairs = N_NA_HEADS // 2
    cb = col0 // LANES
    tables = _na_bias_tables(rpb, rows)
    tq = NA_GROUP * GRID_W

    def case(g):
        return jnp.where(g == 0, 0, jnp.where(g == groups - 1, 2, 1))

    per_step = 2
    assert pairs % per_step == 0 and cb % per_step == 0
    steps, width, cbw = pairs // per_step, per_step * LANES, cb // per_step
    grid_spec = pltpu.PrefetchScalarGridSpec(
        num_scalar_prefetch=1,
        grid=(steps, b, groups),
        in_specs=[pl.BlockSpec((1, tq, width), lambda p, bi, g, bm: (bi, g, cbw + p)),
                  pl.BlockSpec((1, s, width), lambda p, bi, g, bm: (bi, 0, cbw + steps + p)),
                  pl.BlockSpec((1, s, width), lambda p, bi, g, bm: (bi, 0, cbw + 2 * steps + p)),
                  pl.BlockSpec((1, 2 * per_step, tq, NA_WIN * GRID_W), lambda p, bi, g, bm: (case(g), p, 0, 0))],
        out_specs=pl.BlockSpec((1, tq, width), lambda p, bi, g, bm: (bi, g, p)),
        scratch_shapes=[pltpu.SMEM((2 * per_step,), F32)])
    return pl.pallas_call(
        functools.partial(_na_attn_kernel, rows=rows),
        grid_spec=grid_spec,
        out_shape=jax.ShapeDtypeStruct((b, s, N_NA_HEADS * HEAD_DIM), BF16),
        compiler_params=pltpu.CompilerParams(
            dimension_semantics=("arbitrary", "arbitrary", "arbitrary"),
            vmem_limit_bytes=VMEM_LIMIT),
        name="na_attn")(jnp.max(jnp.abs(rpb)).astype(F32).reshape(1), proj, proj, proj, tables)


def _mixer_specs(x_parts, oa, ob, wa, wb, tm, row_tile=lambda i: i):
    x_specs, first = _row_tile_specs(x_parts, tm)
    x_specs = [pl.BlockSpec(s.block_shape, (lambda i, f=s.index_map: f(row_tile(i)))) for s in x_specs]
    specs = x_specs + [pl.BlockSpec((tm, oa.shape[1]), lambda i: (row_tile(i), 0)),
                       pl.BlockSpec((tm, ob.shape[1]), lambda i: (row_tile(i), 0)),
                       pl.BlockSpec(wa.shape, lambda i: (0, 0)),
                       pl.BlockSpec(wb.shape, lambda i: (0, 0))]
    return specs, first


def _mixer_residual(x_refs, first, a_ref, b_ref, wa_ref, wb_ref):
    return (_row_tile(x_refs, first)
            + jnp.dot(a_ref[...], wa_ref[...], preferred_element_type=F32)
            + jnp.dot(b_ref[...], wb_ref[...], preferred_element_type=F32))


def _out_proj_kernel(*refs, n_x, first):
    refs[-1][...] = _mixer_residual(refs[:n_x], first, *refs[n_x:n_x + 4])


def out_proj(x_parts, oa, ob, wa, wb, tm=512):
    n, d = oa.shape[0], wa.shape[1]
    mixer_specs, first = _mixer_specs(x_parts, oa, ob, wa, wb, tm)
    return pl.pallas_call(
        functools.partial(_out_proj_kernel, n_x=len(x_parts), first=first),
        grid=(n // tm,),
        in_specs=mixer_specs,
        out_specs=pl.BlockSpec((tm, d), lambda i: (i, 0)),
        out_shape=jax.ShapeDtypeStruct((n, d), F32),
        compiler_params=pltpu.CompilerParams(dimension_semantics=("parallel",),
                                             vmem_limit_bytes=VMEM_LIMIT),
        name="out_proj")(*x_parts, oa, ob, wa, wb)


def _swiglu_act(a, b):
    return (a * jax.nn.sigmoid(a) * b).astype(BF16)


def _ffn_dense_kernel(x_ref, g_ref, w1_ref, w3_ref, w2_ref, gf_ref, o_ref, *, chunk, final_norm):
    x = x_ref[...]
    h = _rms(x, g_ref[...]).astype(BF16)
    y = x
    for c in range(w1_ref.shape[1] // chunk):
        cols = slice(c * chunk, (c + 1) * chunk)
        act = _swiglu_act(jnp.dot(h, w1_ref[:, cols], preferred_element_type=F32),
                          jnp.dot(h, w3_ref[:, cols], preferred_element_type=F32))
        y = y + jnp.dot(act, w2_ref[cols, :], preferred_element_type=F32)
    o_ref[...] = _rms(y, gf_ref[...]) if final_norm else y


def ffn_dense(x, g, w1, w3, w2, g_final, final_norm, tm=512):
    n, d = x.shape
    f = w1.shape[1]
    chunk = f // 2 if (f // 2) % LANES == 0 else f
    resident = dict(pipeline_mode=pl.Buffered(1))
    return pl.pallas_call(
        functools.partial(_ffn_dense_kernel, chunk=chunk, final_norm=final_norm),
        grid=(n // tm,),
        in_specs=[pl.BlockSpec((tm, d), lambda i: (i, 0)),
                  pl.BlockSpec((1, d), lambda i: (0, 0)),
                  pl.BlockSpec((d, f), lambda i: (0, 0), **resident),
                  pl.BlockSpec((d, f), lambda i: (0, 0), **resident),
                  pl.BlockSpec((f, d), lambda i: (0, 0), **resident),
                  pl.BlockSpec((1, d), lambda i: (0, 0))],
        out_specs=pl.BlockSpec((tm, d), lambda i: (i, 0)),
        out_shape=jax.ShapeDtypeStruct((n, d), F32),
        compiler_params=pltpu.CompilerParams(dimension_semantics=("parallel",),
                                             vmem_limit_bytes=VMEM_LIMIT),
        name="ffn_dense")(x, g, w1, w3, w2, g_final)


MOE_TT = 512
MOE_CH = 16
MOE_R = 2 * MOE_TT + LANES
MOE_LCH = MOE_R // MOE_CH


def _moe_route_kernel(*refs, n_x, first, n_tiles):
    g_ref, rt_ref, xo_ref, stage_ref, cnt_ref, info_ref = refs[n_x + 4:]
    tt = xo_ref.shape[0]
    ne = rt_ref.shape[0]

    @pl.when(pl.program_id(0) == n_tiles)
    def _zero_tile():
        stage_ref[...] = jnp.zeros(stage_ref.shape, BF16)
        cnt_ref[...] = jnp.zeros(cnt_ref.shape, F32)
        info_ref[...] = jnp.zeros(info_ref.shape, F32)

    @pl.when(pl.program_id(0) < n_tiles)
    def _route():
        x = _mixer_residual(refs[:n_x], first, *refs[n_x:n_x + 4])
        xo_ref[...] = x
        h = _rms(x, g_ref[...]).astype(BF16)
        logits = lax.dot_general(rt_ref[...], h, _NT, preferred_element_type=F32)
        erow = lax.broadcasted_iota(jnp.int32, (ne, tt), 0)
        v1 = jnp.max(logits, axis=0, keepdims=True)
        i1 = jnp.min(jnp.where(logits == v1, erow, ne), axis=0, keepdims=True)
        rest = jnp.where(erow == i1, -jnp.inf, logits)
        v2 = jnp.max(rest, axis=0, keepdims=True)
        i2 = jnp.min(jnp.where(rest == v2, erow, ne), axis=0, keepdims=True)
        e2 = jnp.exp(v2 - v1)
        g1 = 1.0 / (1.0 + e2)
        g2 = e2 / (1.0 + e2)
        member = jnp.where((erow == i1) | (erow == i2), 1.0, 0.0)
        before = (lax.broadcasted_iota(jnp.int32, (tt, tt), 0)
                  < lax.broadcasted_iota(jnp.int32, (tt, tt), 1))
        cum = jnp.dot(member.astype(BF16), jnp.where(before, 1.0, 0.0).astype(BF16),
                      preferred_element_type=F32)
        cnt = jnp.sum(member, axis=1, keepdims=True)
        padded = jnp.floor((cnt + (MOE_CH - 1)) * (1.0 / MOE_CH)) * MOE_CH
        padded_b = jnp.broadcast_to(padded, (ne, tt))
        offset = jnp.zeros((1, tt), F32)
        rank1 = jnp.zeros((1, tt), F32)
        rank2 = jnp.zeros((1, tt), F32)
        for e in range(ne):
            pos = offset + cum[e:e + 1]
            rank1 = rank1 + jnp.where(i1 == e, pos, 0.0)
            rank2 = rank2 + jnp.where(i2 == e, pos, 0.0)
            offset = offset + padded_b[e:e + 1]
        r = lax.broadcasted_iota(jnp.int32, (MOE_R, tt), 0).astype(F32)
        perm = jnp.where((r == rank1) | (r == rank2), 1.0, 0.0).astype(BF16)
        stage_ref[0] = jnp.dot(perm, h, preferred_element_type=F32).astype(BF16)
        cnt_ref[0] = jnp.broadcast_to(padded, (ne, LANES))
        irow = lax.broadcasted_iota(jnp.int32, (8, tt), 0)
        info_ref[0] = jnp.where(irow == 0, rank1, jnp.where(irow == 1, rank2,
                                jnp.where(irow == 2, g1, jnp.where(irow == 3, g2, 0.0))))


def moe_route(x_parts, oa, ob, wa, wb, g, router_t):
    n, d = oa.shape[0], wa.shape[1]
    ne = router_t.shape[0]
    assert ne == 8 and n % MOE_TT == 0
    nt = n // MOE_TT

    def row_tile(i):
        return jnp.minimum(i, nt - 1)

    mixer_specs, first = _mixer_specs(x_parts, oa, ob, wa, wb, MOE_TT, row_tile)
    return pl.pallas_call(
        functools.partial(_moe_route_kernel, n_x=len(x_parts), first=first, n_tiles=nt),
        grid=(nt + 1,),
        in_specs=mixer_specs + [pl.BlockSpec((1, d), lambda i: (0, 0)),
                                pl.BlockSpec((ne, d), lambda i: (0, 0))],
        out_specs=[pl.BlockSpec((MOE_TT, d), lambda i: (row_tile(i), 0)),
                   pl.BlockSpec((1, MOE_R, d), lambda i: (i, 0, 0)),
                   pl.BlockSpec((1, ne, LANES), lambda i: (i, 0, 0)),
                   pl.BlockSpec((1, 8, MOE_TT), lambda i: (i, 0, 0))],
        out_shape=[jax.ShapeDtypeStruct((n, d), F32),
                   jax.ShapeDtypeStruct((nt + 1, MOE_R, d), BF16),
                   jax.ShapeDtypeStruct((nt + 1, ne, LANES), F32),
                   jax.ShapeDtypeStruct((nt + 1, 8, MOE_TT), F32)],
        compiler_params=pltpu.CompilerParams(dimension_semantics=("arbitrary",),
                                             vmem_limit_bytes=VMEM_LIMIT),
        name="moe_route")(*x_parts, oa, ob, wa, wb, g, router_t)


def _moe_tables(padded_counts, tm):
    nt, ne = padded_counts.shape
    tile_ch = tm // MOE_CH
    lc = padded_counts // MOE_CH
    loc = jnp.cumsum(lc, axis=1) - lc
    per_expert = jnp.sum(lc, axis=0)
    per_expert_pad = (per_expert + tile_ch - 1) // tile_ch * tile_ch
    expert_end = jnp.cumsum(per_expert_pad)
    expert_base = expert_end - per_expert_pad
    seg_dst = expert_base[None, :] + jnp.cumsum(lc, axis=0) - lc
    max_chunks = (2 * nt * MOE_TT) // MOE_CH + nt * ne + ne * (tile_ch - 1)
    n_tiles = -(-max_chunks // tile_ch)
    n_active = (expert_end[-1] // tile_ch).astype(jnp.int32)
    tile_expert = jnp.minimum(jnp.sum(jnp.arange(n_tiles)[:, None] >= (expert_end // tile_ch)[None, :], axis=1),
                              ne - 1).astype(jnp.int32)
    start = seg_dst.reshape(-1)
    length = lc.reshape(-1)
    source = (jnp.arange(nt)[:, None] * MOE_LCH + loc).reshape(-1)
    d = jnp.arange(n_tiles * tile_ch)[:, None]
    hit = (d >= start[None, :]) & (d < (start + length)[None, :])
    zero_chunk = nt * MOE_LCH
    ffn_src = jnp.sum(jnp.where(hit, (source - start)[None, :] + d, 0), axis=1)
    ffn_src = jnp.where(jnp.any(hit, axis=1), ffn_src, zero_chunk).astype(jnp.int32)
    l = jnp.arange(MOE_LCH)[None, :, None]
    own = (l >= loc[:, None, :]) & (l < (loc + lc)[:, None, :])
    cmb = jnp.sum(jnp.where(own, (seg_dst - loc)[:, None, :] + l, 0), axis=2).astype(jnp.int32)
    n_valid = jnp.sum(lc, axis=1).astype(jnp.int32)
    return tile_expert, n_active.reshape(1), ffn_src, cmb.reshape(-1), n_valid, n_tiles


def _chunk_copies(table_ref, first, count, src_hbm, dst_ref, sem, start, offset=0):
    def one(c):
        row = pl.multiple_of(table_ref[first + c] * MOE_CH, MOE_CH)
        copy = pltpu.make_async_copy(src_hbm.at[pl.ds(row, MOE_CH), :],
                                     dst_ref.at[pl.ds(pl.multiple_of(c * MOE_CH, MOE_CH), MOE_CH), :], sem)
        if start:
            copy.start()
        else:
            copy.wait()

    group = 8 if isinstance(count, int) and count % 8 == 0 else 1

    def body(g, carry):
        for u in range(group):
            one(offset + g * group + u)
        return carry
    lax.fori_loop(0, count // group, body, 0)


def _moe_ffn_kernel(te_ref, nact_ref, src_ref, stage_hbm, w1_ref, w3_ref, w2_ref, o_ref,
                    xbuf, sem, acc_ref, *, tm):
    i = pl.program_id(0)
    j = pl.program_id(1)
    nf = pl.num_programs(1)
    nch = tm // MOE_CH
    nact = nact_ref[0]
    slot = i % 2

    @pl.when((i == 0) & (j == 0))
    def _prime():
        _chunk_copies(src_ref, 0, nch, stage_hbm, xbuf.at[0], sem.at[0], True)

    @pl.when((i < nact) & (j == 0))
    def _rotate():
        _chunk_copies(src_ref, i * nch, nch, stage_hbm, xbuf.at[slot], sem.at[slot], False)

        @pl.when(i + 1 < nact)
        def _prefetch():
            _chunk_copies(src_ref, (i + 1) * nch, nch, stage_hbm, xbuf.at[1 - slot], sem.at[1 - slot], True)

    @pl.when(i < nact)
    def _compute():
        @pl.when(j == 0)
        def _first():
            acc_ref[...] = jnp.zeros(acc_ref.shape, F32)

        x = xbuf[slot]
        tf = w1_ref.shape[2]
        sub = 2 * LANES if tf % (2 * LANES) == 0 else tf
        for c in range(tf // sub):
            cols = slice(c * sub, (c + 1) * sub)
            act = _swiglu_act(jnp.dot(x, w1_ref[0, :, cols], preferred_element_type=F32),
                              jnp.dot(x, w3_ref[0, :, cols], preferred_element_type=F32))
            acc_ref[...] += jnp.dot(act, w2_ref[0, cols, :], preferred_element_type=F32)

        @pl.when(j == nf - 1)
        def _store():
            o_ref[...] = acc_ref[...].astype(BF16)

    @pl.when((i >= nact) & (j == nf - 1))
    def _unused_tile():
        o_ref[...] = jnp.zeros(o_ref.shape, BF16)


def moe_ffn(stage, tile_expert, n_active, ffn_src, w1, w3, w2, n_tiles, tm):
    d = stage.shape[-1]
    f = w1.shape[2]
    tf = next((c for c in (14 * LANES, 4 * LANES) if f % c == 0), f)
    nf = f // tf

    def live(i, na):
        return jnp.minimum(i, na[0] - 1)

    def fcol(i, j, na):
        return jnp.where(i < na[0], j, nf - 1)

    grid_spec = pltpu.PrefetchScalarGridSpec(
        num_scalar_prefetch=3,
        grid=(n_tiles, nf),
        in_specs=[pl.BlockSpec(memory_space=pl.ANY),
                  pl.BlockSpec((1, d, tf), lambda i, j, te, na, src: (te[live(i, na)], 0, fcol(i, j, na))),
                  pl.BlockSpec((1, d, tf), lambda i, j, te, na, src: (te[live(i, na)], 0, fcol(i, j, na))),
                  pl.BlockSpec((1, tf, d), lambda i, j, te, na, src: (te[live(i, na)], fcol(i, j, na), 0))],
        out_specs=pl.BlockSpec((tm, d), lambda i, j, te, na, src: (i, 0)),
        scratch_shapes=[pltpu.VMEM((2, tm, d), BF16),
                        pltpu.SemaphoreType.DMA((2,)),
                        pltpu.VMEM((tm, d), F32)])
    return pl.pallas_call(
        functools.partial(_moe_ffn_kernel, tm=tm),
        grid_spec=grid_spec,
        out_shape=jax.ShapeDtypeStruct((n_tiles * tm, d), BF16),
        compiler_params=pltpu.CompilerParams(dimension_semantics=("arbitrary", "arbitrary"),
                                             vmem_limit_bytes=VMEM_LIMIT),
        name="moe_ffn")(tile_expert, n_active, ffn_src, stage.reshape(-1, d), w1, w3, w2)


def _moe_combine_kernel(cmb_ref, nval_ref, x_ref, info_ref, ys_hbm, gf_ref, *refs, first, final_norm):
    o_refs, (ybuf, sem) = refs[:-2], refs[-2:]
    t = pl.program_id(0)
    nt = pl.num_programs(0)
    slot = t % 2

    def fetch(tile, slot, start):
        nv = nval_ref[tile]
        sure = 2 * MOE_TT // MOE_CH
        _chunk_copies(cmb_ref, tile * MOE_LCH, sure, ys_hbm, ybuf.at[slot], sem.at[slot], start)
        _chunk_copies(cmb_ref, tile * MOE_LCH, nv - sure, ys_hbm, ybuf.at[slot], sem.at[slot], start, sure)
        if start:
            def clear(l, carry):
                ybuf[slot, pl.ds(pl.multiple_of(l * MOE_CH, MOE_CH), MOE_CH), :] = jnp.zeros(
                    (MOE_CH, ybuf.shape[2]), BF16)
                return carry
            lax.fori_loop(nv, MOE_LCH, clear, 0)

    @pl.when(t == 0)
    def _prime():
        fetch(0, 0, True)

    fetch(t, slot, False)

    @pl.when(t + 1 < nt)
    def _prefetch():
        fetch(t + 1, 1 - slot, True)

    info = info_ref[...]
    tt = info.shape[0]
    col = lax.broadcasted_iota(jnp.int32, (tt, MOE_R), 1).astype(F32)
    pick = jnp.concatenate([jnp.where(col == info[:, 0:1], 1.0, 0.0).astype(BF16),
                            jnp.where(col == info[:, 1:2], 1.0, 0.0).astype(BF16)], axis=0)
    y = jnp.dot(pick, ybuf[slot], preferred_element_type=F32)
    out = x_ref[...] + info[:, 2:3] * y[:tt] + info[:, 3:4] * y[tt:]
    out = _rms(out, gf_ref[...]) if final_norm else out
    if len(o_refs) == 1:
        o_refs[0][...] = out
    else:
        @pl.when(t < first)
        def _head():
            o_refs[0][...] = out

        @pl.when(t >= first)
        def _tail():
            o_refs[1][...] = out


def moe_combine(x, info_cols, ys, cmb, n_valid, g_final, final_norm, row_split):
    n, d = x.shape
    nt = n // MOE_TT
    assert sum(row_split) == n and all(r % MOE_TT == 0 for r in row_split) and len(row_split) <= 2
    first = row_split[0] // MOE_TT
    if len(row_split) == 1:
        out_specs = [pl.BlockSpec((MOE_TT, d), lambda t, c, v: (t, 0))]
    else:
        out_specs = [pl.BlockSpec((MOE_TT, d), lambda t, c, v: (jnp.minimum(t, first - 1), 0)),
                     pl.BlockSpec((MOE_TT, d), lambda t, c, v: (jnp.maximum(t - first, 0), 0))]
    grid_spec = pltpu.PrefetchScalarGridSpec(
        num_scalar_prefetch=2,
        grid=(nt,),
        in_specs=[pl.BlockSpec((MOE_TT, d), lambda t, c, v: (t, 0)),
                  pl.BlockSpec((MOE_TT, 8), lambda t, c, v: (t, 0)),
                  pl.BlockSpec(memory_space=pl.ANY),
                  pl.BlockSpec((1, d), lambda t, c, v: (0, 0))],
        out_specs=out_specs,
        scratch_shapes=[pltpu.VMEM((2, MOE_R, d), BF16),
                        pltpu.SemaphoreType.DMA((2,))])
    return pl.pallas_call(
        functools.partial(_moe_combine_kernel, first=first, final_norm=final_norm),
        grid_spec=grid_spec,
        out_shape=[jax.ShapeDtypeStruct((r, d), F32) for r in row_split],
        compiler_params=pltpu.CompilerParams(dimension_semantics=("arbitrary",),
                                             vmem_limit_bytes=VMEM_LIMIT),
        name="moe_combine")(cmb, n_valid, x, info_cols, ys, g_final)


def moe_top2(x_parts, oa, ob, wa, wb, g, router, w1, w3, w2, g_final, final_norm, row_split, tm=1024):
    n, d = oa.shape[0], wa.shape[1]
    nt = n // MOE_TT
    x, stage, counts, info = moe_route(x_parts, oa, ob, wa, wb, g, router.T.astype(BF16))
    padded_counts = counts[:nt, :, 0].astype(jnp.int32)
    tile_expert, n_active, ffn_src, cmb, n_valid, n_tiles = _moe_tables(padded_counts, tm)
    ys = moe_ffn(stage, tile_expert, n_active, ffn_src, w1, w3, w2, n_tiles, tm)
    info_cols = info[:nt].transpose(0, 2, 1).reshape(n, 8)
    return moe_combine(x, info_cols, ys, cmb, n_valid, g_final, final_norm, row_split)


def _trunk(x_parts, w_in, w_out, norm_mix, norm_ffn, lambda_qk, subln_g, na_rpb,
           ffn_w1, ffn_w3, ffn_w2, moe_router, moe_w1, moe_w3, moe_w2, norm_final):
    s, d = x_parts[0].shape[1:]
    rows = tuple(p.shape[0] * s for p in x_parts)
    b, n = sum(p.shape[0] for p in x_parts), sum(rows)
    depth = w_in.shape[0]
    diff_w = N_DIFF_HEADS * 2 * HEAD_DIM
    na_w = N_NA_HEADS * HEAD_DIM
    scale = HEAD_DIM ** -0.5
    col_scale = jnp.ones((w_in.shape[2],), F32)
    col_scale = col_scale.at[:diff_w].set(scale).at[3 * diff_w:3 * diff_w + na_w].set(scale)
    xs = [p.reshape(-1, d) for p in x_parts]
    for l in range(depth):
        proj = norm_proj(xs, norm_mix[l].reshape(1, d), (w_in[l] * col_scale).astype(BF16))
        proj = proj.reshape(b, s, -1)
        oa = diff_attn(proj, lambda_qk[l], subln_g[l], l).reshape(n, diff_w)
        ob = na_attn(proj, na_rpb[l], 3 * diff_w).reshape(n, na_w)
        wo = w_out[l].astype(BF16)
        mixed = (xs, oa, ob, wo[:diff_w], wo[diff_w:])
        last = l == depth - 1
        i = l // 2
        if l % 2 == 0:
            xs = [ffn_dense(out_proj(*mixed), norm_ffn[l].reshape(1, d), ffn_w1[i].astype(BF16),
                            ffn_w3[i].astype(BF16), ffn_w2[i].astype(BF16),
                            norm_final.reshape(1, d), final_norm=last)]
        else:
            xs = moe_top2(*mixed, norm_ffn[l].reshape(1, d), moe_router[i], moe_w1[i].astype(BF16),
                          moe_w3[i].astype(BF16), moe_w2[i].astype(BF16),
                          norm_final.reshape(1, d), final_norm=last,
                          row_split=rows if last else (n,))
    if len(xs) != len(rows):
        bounds = np.cumsum((0,) + rows)
        xs = [xs[0][lo:hi] for lo, hi in zip(bounds[:-1], bounds[1:])]
    return tuple(y.reshape(-1, s, d) for y in xs)


def kernel(x_prompt, x_sample, w_in, w_out, norm_mix, norm_ffn, lambda_qk, subln_g, na_rpb,
           ffn_w1, ffn_w3, ffn_w2, moe_router, moe_w1, moe_w3, moe_w2, norm_final):
    assert x_prompt.shape[1:] == x_sample.shape[1:]
    return _trunk([x_prompt, x_sample], w_in, w_out, norm_mix, norm_ffn, lambda_qk, subln_g, na_rpb,
                  ffn_w1, ffn_w3, ffn_w2, moe_router, moe_w1, moe_w3, moe_w2, norm_final)
```

```python
import functools
import math

import numpy as np
import jax
import jax.numpy as jnp
from jax import lax
from jax.experimental import pallas as pl
from jax.experimental.pallas import tpu as pltpu

F32 = jnp.float32
BF16 = jnp.bfloat16

RMS_EPS = 1e-5
HEAD_DIM = 64
N_DIFF_HEADS = 4
N_NA_HEADS = 8
GRID_W = 64
NA_KH = 8
NA_KW = 16
NA_GROUP = 8
NA_WIN = 16
TOP_K = 2
LANES = 128
NEG = -1e30
VMEM_LIMIT = 56 * 1024 * 1024

_NT = (((1,), (1,)), ((), ()))


def _rms(x, g):
    return x * lax.rsqrt(jnp.mean(x * x, axis=-1, keepdims=True) + RMS_EPS) * g


def _row_tile_specs(parts, tm):
    first = parts[0].shape[0] // tm
    d = parts[0].shape[1]
    specs = [pl.BlockSpec((tm, d), lambda i: (jnp.minimum(i, first - 1), 0))]
    if len(parts) == 2:
        assert parts[0].shape[0] % tm == 0 and parts[1].shape[0] % tm == 0
        specs.append(pl.BlockSpec((tm, d), lambda i: (jnp.maximum(i - first, 0), 0)))
    return specs, first


def _row_tile(x_refs, first):
    if len(x_refs) == 1:
        return x_refs[0][...]
    return jnp.where(pl.program_id(0) < first, x_refs[0][...], x_refs[1][...])


def _norm_proj_kernel(*refs, n_x, first, chunk):
    g_ref, w_ref, o_ref = refs[n_x:]
    h = _rms(_row_tile(refs[:n_x], first), g_ref[...]).astype(BF16)
    for c in range(o_ref.shape[1] // chunk):
        cols = slice(c * chunk, (c + 1) * chunk)
        o_ref[:, cols] = jnp.dot(h, w_ref[:, cols], preferred_element_type=F32).astype(BF16)


def norm_proj(x_parts, g, w, tm=512):
    n = sum(p.shape[0] for p in x_parts)
    d, c = w.shape
    x_specs, first = _row_tile_specs(x_parts, tm)
    return pl.pallas_call(
        functools.partial(_norm_proj_kernel, n_x=len(x_parts), first=first, chunk=1024),
        grid=(n // tm,),
        in_specs=x_specs + [pl.BlockSpec((1, d), lambda i: (0, 0)),
                            pl.BlockSpec((d, c), lambda i: (0, 0))],
        out_specs=pl.BlockSpec((tm, c), lambda i: (i, 0)),
        out_shape=jax.ShapeDtypeStruct((n, c), BF16),
        compiler_params=pltpu.CompilerParams(dimension_semantics=("parallel",),
                                             vmem_limit_bytes=VMEM_LIMIT),
        name="norm_proj")(*x_parts, g, w)


SCORE_CAP = 40.0
EXP_ZERO = 88.0


def _diff_attn_kernel(slopes_ref, q_ref, k_ref, v_ref, lam_ref, g_ref, o_ref,
                      k1_ref, k2_ref, va_ref, qv_ref, dist_ref, acc_ref, m_ref, kn_ref,
                      *, t, whole, lambda_init):
    h = pl.program_id(1)
    i = pl.program_id(2)
    n = k_ref.shape[1] // t
    slope = slopes_ref[h]
    lane = lax.broadcasted_iota(jnp.int32, (t, LANES), 1)
    low = lane < HEAD_DIM
    centred = (lax.broadcasted_iota(jnp.int32, (t, LANES), 0) - t // 2).astype(F32)

    def extras(base, a, b, c, d):
        return jnp.where(lane == base, a, jnp.where(lane == base + 1, b,
                         jnp.where(lane == base + 2, c, jnp.where(lane == base + 3, d, 0.0))))

    @pl.when(i == 0)
    def _prepare_keys():
        def body(j, norms):
            rows = pl.ds(pl.multiple_of(j * t, t), t)
            k = k_ref[0, rows, :].astype(F32)
            base = slope * jnp.asarray(j * t).astype(F32)
            k1_ref[rows, :] = jnp.where(low, k, extras(HEAD_DIM, -slope, slope * centred, 1.0, base)).astype(BF16)
            k2_ref[rows, :] = jnp.where(low, extras(0, -slope, slope * centred, 1.0, base), k).astype(BF16)
            va_ref[rows, :LANES] = v_ref[0, rows, :]
            va_ref[rows, LANES:] = jnp.ones((t, LANES), BF16)
            return jnp.maximum(norms, jnp.dot(k * k, half_sums, preferred_element_type=F32))

        li = lax.broadcasted_iota(jnp.int32, (LANES, LANES), 0)
        ci = lax.broadcasted_iota(jnp.int32, (LANES, LANES), 1)
        half_sums = jnp.where(((ci == 0) & (li < HEAD_DIM)) | ((ci == 1) & (li >= HEAD_DIM)), 1.0, 0.0)
        norms = lax.fori_loop(0, n, body, jnp.zeros((t, LANES), F32))
        kn_ref[0] = jnp.max(norms[:, 0:1])
        kn_ref[1] = jnp.max(norms[:, 1:2])
        dist_ref[...] = slope * jnp.abs(lax.broadcasted_iota(jnp.int32, (t, t), 0)
                                        - lax.broadcasted_iota(jnp.int32, (t, t), 1)).astype(F32)

    q = q_ref[0].astype(F32)
    qbase = -slope * jnp.asarray(i * t).astype(F32)
    for si, sign in enumerate((1.0, 0.0, -1.0)):
        qv_ref[0, si] = jnp.where(low, q, sign * extras(HEAD_DIM, centred, 1.0, qbase, 1.0)).astype(BF16)
        qv_ref[1, si] = jnp.where(low, sign * extras(0, centred, 1.0, qbase, 1.0), q).astype(BF16)
    before_q, diag_q, after_q = 0, 1, 2

    q2 = q * q
    u2 = jnp.maximum(jnp.sum(jnp.where(low, q2, 0.0), axis=1, keepdims=True) * kn_ref[0],
                     jnp.sum(jnp.where(low, 0.0, q2), axis=1, keepdims=True) * kn_ref[1])
    u = 1.01 * jnp.sqrt(jnp.max(u2, axis=0, keepdims=True))
    reach = jnp.floor((EXP_ZERO + 2.0 * u) / (slope * t)) + 1.0
    far = 1e6
    code = jnp.max(jnp.where(u <= SCORE_CAP, reach, far))
    unshifted = code < far
    reach = jnp.where(unshifted, code.astype(jnp.int32), n)
    j_lo = jnp.maximum(i - reach, 0)
    j_hi = jnp.minimum(i + reach, n - 1)

    acc_ref[...] = jnp.zeros(acc_ref.shape, F32)

    def scores(mi, qi, rows):
        km_ref = k1_ref if mi == 0 else k2_ref
        s = lax.dot_general(qv_ref[mi, qi], km_ref[rows, :], _NT, preferred_element_type=F32)
        if qi == diag_q:
            s = s - dist_ref[...]
        return s

    def plain_tile(j, qi, width=1):
        rows = pl.ds(pl.multiple_of(j * t, t), width * t)
        va = va_ref[rows, :]
        for mi in range(2):
            p = jnp.exp(scores(mi, qi, rows)).astype(BF16)
            acc_ref[mi] += jnp.dot(p, va, preferred_element_type=F32)

    def online_tile(j, qi, width=1):
        rows = pl.ds(pl.multiple_of(j * t, t), width * t)
        va = va_ref[rows, :]
        for mi in range(2):
            s = scores(mi, qi, rows)
            m_old = m_ref[mi]
            m_new = jnp.maximum(m_old, jnp.max(s, axis=1, keepdims=True))
            p = jnp.exp(s - m_new)
            acc_ref[mi] = (jnp.exp(m_old - m_new) * acc_ref[mi]
                           + jnp.dot(p.astype(BF16), va, preferred_element_type=F32))
            m_ref[mi] = m_new

    def walk(tile, width):
        def run(lo, hi, qi):
            odd = (hi - lo) % width
            for r in range(width - 1):
                @pl.when(r < odd)
                def _single():
                    tile(lo + r, qi)

            def body(c, carry):
                tile(lo + odd + c * width, qi, width)
                return carry
            lax.fori_loop(0, (hi - lo) // width, body, 0)

        run(j_lo, i, before_q)
        tile(i, diag_q)
        run(i + 1, j_hi + 1, after_q)

    def whole_row_step(c, carry):
        for mi in range(2):
            km_ref = k1_ref if mi == 0 else k2_ref
            parts = []
            for w in range(whole):
                j = c * whole + w
                rows = pl.ds(pl.multiple_of(j * t, t), t)
                s = lax.dot_general(qv_ref[mi, jnp.clip(j - i, -1, 1) + 1], km_ref[rows, :], _NT,
                                    preferred_element_type=F32)
                s = s - jnp.where(j == i, 1.0, 0.0) * dist_ref[...]
                parts.append(jnp.dot(jnp.exp(s).astype(BF16), va_ref[rows, :], preferred_element_type=F32))
            acc_ref[mi] += sum(parts)
        return carry

    every_tile = (j_lo == 0) & (j_hi == n - 1)

    @pl.when(unshifted & every_tile)
    def _plain_whole_row():
        lax.fori_loop(0, n // whole, whole_row_step, 0)

    @pl.when(unshifted & jnp.logical_not(every_tile))
    def _plain():
        walk(plain_tile, 2)

    @pl.when(jnp.logical_not(unshifted))
    def _online():
        m_ref[...] = jnp.full(m_ref.shape, NEG, F32)
        walk(online_tile, 1)

    a1 = acc_ref[0]
    a2 = acc_ref[1]
    lf = lam_ref[...]
    lam_full = (jnp.exp(jnp.sum(lf[0:1] * lf[1:2], axis=1, keepdims=True))
                - jnp.exp(jnp.sum(lf[2:3] * lf[3:4], axis=1, keepdims=True)) + lambda_init)
    o = a1[:, :LANES] / a1[:, LANES:] - lam_full * (a2[:, :LANES] / a2[:, LANES:])
    o_ref[0] = (_rms(o, g_ref[...]) * (1.0 - lambda_init)).astype(BF16)


def diff_attn(proj, lam, subln_g, layer_idx, t=512):
    b, s, _ = proj.shape
    nh = N_DIFF_HEADS
    assert (8 % nh) == 0 and t <= 512 and (t & (t - 1)) == 0
    slopes = jnp.asarray([2.0 ** (-8.0 * (h + 1) / nh) for h in range(nh)], F32)
    lambda_init = 0.8 - 0.6 * math.exp(-0.3 * layer_idx)
    grid_spec = pltpu.PrefetchScalarGridSpec(
        num_scalar_prefetch=1,
        grid=(b, nh, s // t),
        in_specs=[pl.BlockSpec((1, t, LANES), lambda bi, h, i, sl: (bi, i, h)),
                  pl.BlockSpec((1, s, LANES), lambda bi, h, i, sl: (bi, 0, nh + h)),
                  pl.BlockSpec((1, s, LANES), lambda bi, h, i, sl: (bi, 0, 2 * nh + h)),
                  pl.BlockSpec((4, HEAD_DIM), lambda bi, h, i, sl: (0, 0)),
                  pl.BlockSpec((1, 2 * HEAD_DIM), lambda bi, h, i, sl: (0, 0))],
        out_specs=pl.BlockSpec((1, t, LANES), lambda bi, h, i, sl: (bi, i, h)),
        scratch_shapes=[pltpu.VMEM((s, LANES), BF16),
                        pltpu.VMEM((s, LANES), BF16),
                        pltpu.VMEM((s, 2 * LANES), BF16),
                        pltpu.VMEM((2, 3, t, LANES), BF16),
                        pltpu.VMEM((t, t), F32),
                        pltpu.VMEM((2, t, 2 * LANES), F32),
                        pltpu.VMEM((2, t, 1), F32),
                        pltpu.SMEM((2,), F32)])
    whole = math.gcd(s // t, 16)
    return pl.pallas_call(
        functools.partial(_diff_attn_kernel, t=t, whole=whole, lambda_init=lambda_init),
        grid_spec=grid_spec,
        out_shape=jax.ShapeDtypeStruct((b, s, nh * 2 * HEAD_DIM), BF16),
        compiler_params=pltpu.CompilerParams(
            dimension_semantics=("arbitrary", "arbitrary", "arbitrary"),
            vmem_limit_bytes=VMEM_LIMIT),
        name="diff_attn")(slopes, proj, proj, proj, lam, subln_g.reshape(1, -1))


def _na_window_start(group, rows):
    return jnp.clip(group * NA_GROUP - NA_KH // 2, 0, rows - NA_WIN)


def _na_bias_tables(rpb, rows):
    n_rho, n_chi = 2 * NA_KH - 1, 2 * NA_KW - 1
    c = np.arange(GRID_W)[:, None]
    kc = np.arange(GRID_W)[None, :]
    cs = np.clip(c - NA_KW // 2, 0, GRID_W - NA_KW)
    col_ok = (kc >= cs) & (kc < cs + NA_KW)
    col_sel = ((kc - c + NA_KW - 1)[..., None] == np.arange(n_chi)) & col_ok[..., None]
    qr = np.arange(NA_GROUP)[:, None]
    kr = np.arange(NA_WIN)[None, :]
    row_sel, row_ok = [], []
    for r0 in (0, NA_GROUP, rows - NA_GROUP):
        ws = min(max(r0 - NA_KH // 2, 0), rows - NA_WIN)
        r = r0 + qr
        rs = np.clip(r - NA_KH // 2, 0, rows - NA_KH)
        ok = (ws + kr >= rs) & (ws + kr < rs + NA_KH)
        row_ok.append(ok)
        row_sel.append(((ws + kr - r + NA_KH - 1)[..., None] == np.arange(n_rho)) & ok[..., None])
    row_sel, row_ok = np.stack(row_sel), np.stack(row_ok)
    by_col = jnp.einsum("hrd,ckd->hrck", rpb.astype(F32), col_sel.astype(np.float32), precision="highest")
    bias = jnp.einsum("hrck,gqnr->ghqcnk", by_col, row_sel.astype(np.float32), precision="highest")
    valid = row_ok[:, None, :, None, :, None] & col_ok[None, None, None, :, None, :]
    shape = (3, rpb.shape[0], NA_GROUP * GRID_W, NA_WIN * GRID_W)
    return jnp.where(valid, bias, NEG).reshape(shape)


def _na_attn_kernel(bmax_ref, q_ref, k_ref, v_ref, t_ref, o_ref, kn_ref, *, rows):
    g = pl.program_id(2)
    nwin = NA_WIN * GRID_W
    tq = q_ref.shape[1]
    n_pairs = q_ref.shape[2] // LANES
    low = lax.broadcasted_iota(jnp.int32, (nwin, LANES), 1) < HEAD_DIM
    low_q = lax.broadcasted_iota(jnp.int32, (tq, LANES), 1) < HEAD_DIM

    @pl.when(g == 0)
    def _key_norms():
        li = lax.broadcasted_iota(jnp.int32, (LANES, LANES), 0)
        ci = lax.broadcasted_iota(jnp.int32, (LANES, LANES), 1)
        half_sums = jnp.where(((ci == 0) & (li < HEAD_DIM)) | ((ci == 1) & (li >= HEAD_DIM)), 1.0, 0.0)
        for pi in range(n_pairs):
            lanes = slice(pi * LANES, (pi + 1) * LANES)

            def body(c, norms):
                k = k_ref[0, pl.ds(pl.multiple_of(c * nwin, nwin), nwin), lanes].astype(F32)
                return jnp.maximum(norms, jnp.dot(k * k, half_sums, preferred_element_type=F32))

            norms = lax.fori_loop(0, k_ref.shape[1] // nwin, body, jnp.zeros((nwin, LANES), F32))
            kn_ref[2 * pi] = jnp.max(norms[:, 0:1])
            kn_ref[2 * pi + 1] = jnp.max(norms[:, 1:2])

    u2 = jnp.zeros((tq, 1), F32)
    for pi in range(n_pairs):
        qf = q_ref[0, :, pi * LANES:(pi + 1) * LANES].astype(F32)
        q2 = qf * qf
        u2 = jnp.maximum(u2, jnp.maximum(
            jnp.sum(jnp.where(low_q, q2, 0.0), axis=1, keepdims=True) * kn_ref[2 * pi],
            jnp.sum(jnp.where(low_q, 0.0, q2), axis=1, keepdims=True) * kn_ref[2 * pi + 1]))
    unshifted = jnp.max(1.01 * jnp.sqrt(jnp.max(u2, axis=0, keepdims=True))) + bmax_ref[0] <= SCORE_CAP

    start = pl.multiple_of(_na_window_start(g, rows) * GRID_W, 4 * GRID_W)
    win = pl.ds(start, nwin)

    def head_pair(pi, plain):
        lanes = slice(pi * LANES, (pi + 1) * LANES)
        kw = k_ref[0, win, lanes]
        vw = v_ref[0, win, lanes]
        q = q_ref[0, :, lanes]
        zero = jnp.zeros_like(kw)
        out = None
        for hi, keep in enumerate((low, ~low)):
            s = (lax.dot_general(q, jnp.where(keep, kw, zero), _NT, preferred_element_type=F32)
                 + t_ref[0, 2 * pi + hi])
            if plain:
                values = jnp.concatenate([jnp.where(keep, vw, zero), jnp.ones(vw.shape, BF16)], axis=1)
                r = jnp.dot(jnp.exp(s).astype(BF16), values, preferred_element_type=F32)
                o = r[:, :LANES] / r[:, LANES:]
            else:
                p = jnp.exp(s - jnp.max(s, axis=1, keepdims=True))
                l = jnp.sum(p, axis=1, keepdims=True)
                o = jnp.dot(p.astype(BF16), jnp.where(keep, vw, zero), preferred_element_type=F32) / l
            out = o if out is None else out + o
        o_ref[0, :, lanes] = out.astype(BF16)

    @pl.when(unshifted)
    def _plain():
        for pi in range(n_pairs):
            head_pair(pi, True)

    @pl.when(jnp.logical_not(unshifted))
    def _shifted():
        for pi in range(n_pairs):
            head_pair(pi, False)


def na_attn(proj, rpb, col0):
    b, s, _ = proj.shape
    rows = s // GRID_W
    assert rows % NA_WIN == 0 and rows >= 2 * NA_WIN
    groups = rows // NA_GROUP
    pairs = N_NA_HEADS // 2
    cb = col0 // LANES
    tables = _na_bias_tables(rpb, rows)
    tq = NA_GROUP * GRID_W

    def case(g):
        return jnp.where(g == 0, 0, jnp.where(g == groups - 1, 2, 1))

    per_step = 2
    assert pairs % per_step == 0 and cb % per_step == 0
    steps, width, cbw = pairs // per_step, per_step * LANES, cb // per_step
    grid_spec = pltpu.PrefetchScalarGridSpec(
        num_scalar_prefetch=1,
        grid=(steps, b, groups),
        in_specs=[pl.BlockSpec((1, tq, width), lambda p, bi, g, bm: (bi, g, cbw + p)),
                  pl.BlockSpec((1, s, width), lambda p, bi, g, bm: (bi, 0, cbw + steps + p)),
                  pl.BlockSpec((1, s, width), lambda p, bi, g, bm: (bi, 0, cbw + 2 * steps + p)),
                  pl.BlockSpec((1, 2 * per_step, tq, NA_WIN * GRID_W), lambda p, bi, g, bm: (case(g), p, 0, 0))],
        out_specs=pl.BlockSpec((1, tq, width), lambda p, bi, g, bm: (bi, g, p)),
        scratch_shapes=[pltpu.SMEM((2 * per_step,), F32)])
    return pl.pallas_call(
        functools.partial(_na_attn_kernel, rows=rows),
        grid_spec=grid_spec,
        out_shape=jax.ShapeDtypeStruct((b, s, N_NA_HEADS * HEAD_DIM), BF16),
        compiler_params=pltpu.CompilerParams(
            dimension_semantics=("arbitrary", "arbitrary", "arbitrary"),
            vmem_limit_bytes=VMEM_LIMIT),
        name="na_attn")(jnp.max(jnp.abs(rpb)).astype(F32).reshape(1), proj, proj, proj, tables)


def _mixer_specs(x_parts, oa, ob, wa, wb, tm, row_tile=lambda i: i):
    x_specs, first = _row_tile_specs(x_parts, tm)
    x_specs = [pl.BlockSpec(s.block_shape, (lambda i, f=s.index_map: f(row_tile(i)))) for s in x_specs]
    specs = x_specs + [pl.BlockSpec((tm, oa.shape[1]), lambda i: (row_tile(i), 0)),
                       pl.BlockSpec((tm, ob.shape[1]), lambda i: (row_tile(i), 0)),
                       pl.BlockSpec(wa.shape, lambda i: (0, 0)),
                       pl.BlockSpec(wb.shape, lambda i: (0, 0))]
    return specs, first


def _mixer_residual(x_refs, first, a_ref, b_ref, wa_ref, wb_ref):
    return (_row_tile(x_refs, first)
            + jnp.dot(a_ref[...], wa_ref[...], preferred_element_type=F32)
            + jnp.dot(b_ref[...], wb_ref[...], preferred_element_type=F32))


def _out_proj_kernel(*refs, n_x, first):
    refs[-1][...] = _mixer_residual(refs[:n_x], first, *refs[n_x:n_x + 4])


def out_proj(x_parts, oa, ob, wa, wb, tm=512):
    n, d = oa.shape[0], wa.shape[1]
    mixer_specs, first = _mixer_specs(x_parts, oa, ob, wa, wb, tm)
    return pl.pallas_call(
        functools.partial(_out_proj_kernel, n_x=len(x_parts), first=first),
        grid=(n // tm,),
        in_specs=mixer_specs,
        out_specs=pl.BlockSpec((tm, d), lambda i: (i, 0)),
        out_shape=jax.ShapeDtypeStruct((n, d), F32),
        compiler_params=pltpu.CompilerParams(dimension_semantics=("parallel",),
                                             vmem_limit_bytes=VMEM_LIMIT),
        name="out_proj")(*x_parts, oa, ob, wa, wb)


def _swiglu_act(a, b):
    return (a * jax.nn.sigmoid(a) * b).astype(BF16)


def _ffn_dense_kernel(x_ref, g_ref, w1_ref, w3_ref, w2_ref, gf_ref, o_ref, *, chunk, final_norm):
    x = x_ref[...]
    h = _rms(x, g_ref[...]).astype(BF16)
    y = x
    for c in range(w1_ref.shape[1] // chunk):
        cols = slice(c * chunk, (c + 1) * chunk)
        act = _swiglu_act(jnp.dot(h, w1_ref[:, cols], preferred_element_type=F32),
                          jnp.dot(h, w3_ref[:, cols], preferred_element_type=F32))
        y = y + jnp.dot(act, w2_ref[cols, :], preferred_element_type=F32)
    o_ref[...] = _rms(y, gf_ref[...]) if final_norm else y


def ffn_dense(x, g, w1, w3, w2, g_final, final_norm, tm=512):
    n, d = x.shape
    f = w1.shape[1]
    chunk = f // 2 if (f // 2) % LANES == 0 else f
    resident = dict(pipeline_mode=pl.Buffered(1))
    return pl.pallas_call(
        functools.partial(_ffn_dense_kernel, chunk=chunk, final_norm=final_norm),
        grid=(n // tm,),
        in_specs=[pl.BlockSpec((tm, d), lambda i: (i, 0)),
                  pl.BlockSpec((1, d), lambda i: (0, 0)),
                  pl.BlockSpec((d, f), lambda i: (0, 0), **resident),
                  pl.BlockSpec((d, f), lambda i: (0, 0), **resident),
                  pl.BlockSpec((f, d), lambda i: (0, 0), **resident),
                  pl.BlockSpec((1, d), lambda i: (0, 0))],
        out_specs=pl.BlockSpec((tm, d), lambda i: (i, 0)),
        out_shape=jax.ShapeDtypeStruct((n, d), F32),
        compiler_params=pltpu.CompilerParams(dimension_semantics=("parallel",),
                                             vmem_limit_bytes=VMEM_LIMIT),
        name="ffn_dense")(x, g, w1, w3, w2, g_final)


MOE_TT = 512
MOE_CH = 16
MOE_R = 2 * MOE_TT + LANES
MOE_LCH = MOE_R // MOE_CH


def _moe_route_kernel(*refs, n_x, first, n_tiles):
    g_ref, rt_ref, xo_ref, stage_ref, cnt_ref, info_ref = refs[n_x + 4:]
    tt = xo_ref.shape[0]
    ne = rt_ref.shape[0]

    @pl.when(pl.program_id(0) == n_tiles)
    def _zero_tile():
        stage_ref[...] = jnp.zeros(stage_ref.shape, BF16)
        cnt_ref[...] = jnp.zeros(cnt_ref.shape, F32)
        info_ref[...] = jnp.zeros(info_ref.shape, F32)

    @pl.when(pl.program_id(0) < n_tiles)
    def _route():
        x = _mixer_residual(refs[:n_x], first, *refs[n_x:n_x + 4])
        xo_ref[...] = x
        h = _rms(x, g_ref[...]).astype(BF16)
        logits = lax.dot_general(rt_ref[...], h, _NT, preferred_element_type=F32)
        erow = lax.broadcasted_iota(jnp.int32, (ne, tt), 0)
        v1 = jnp.max(logits, axis=0, keepdims=True)
        i1 = jnp.min(jnp.where(logits == v1, erow, ne), axis=0, keepdims=True)
        rest = jnp.where(erow == i1, -jnp.inf, logits)
        v2 = jnp.max(rest, axis=0, keepdims=True)
        i2 = jnp.min(jnp.where(rest == v2, erow, ne), axis=0, keepdims=True)
        e2 = jnp.exp(v2 - v1)
        g1 = 1.0 / (1.0 + e2)
        g2 = e2 / (1.0 + e2)
        member = jnp.where((erow == i1) | (erow == i2), 1.0, 0.0)
        before = (lax.broadcasted_iota(jnp.int32, (tt, tt), 0)
                  < lax.broadcasted_iota(jnp.int32, (tt, tt), 1))
        cum = jnp.dot(member.astype(BF16), jnp.where(before, 1.0, 0.0).astype(BF16),
                      preferred_element_type=F32)
        cnt = jnp.sum(member, axis=1, keepdims=True)
        padded = jnp.floor((cnt + (MOE_CH - 1)) * (1.0 / MOE_CH)) * MOE_CH
        padded_b = jnp.broadcast_to(padded, (ne, tt))
        offset = jnp.zeros((1, tt), F32)
        rank1 = jnp.zeros((1, tt), F32)
        rank2 = jnp.zeros((1, tt), F32)
        for e in range(ne):
            pos = offset + cum[e:e + 1]
            rank1 = rank1 + jnp.where(i1 == e, pos, 0.0)
            rank2 = rank2 + jnp.where(i2 == e, pos, 0.0)
            offset = offset + padded_b[e:e + 1]
        r = lax.broadcasted_iota(jnp.int32, (MOE_R, tt), 0).astype(F32)
        perm = jnp.where((r == rank1) | (r == rank2), 1.0, 0.0).astype(BF16)
        stage_ref[0] = jnp.dot(perm, h, preferred_element_type=F32).astype(BF16)
        cnt_ref[0] = jnp.broadcast_to(padded, (ne, LANES))
        irow = lax.broadcasted_iota(jnp.int32, (8, tt), 0)
        info_ref[0] = jnp.where(irow == 0, rank1, jnp.where(irow == 1, rank2,
                                jnp.where(irow == 2, g1, jnp.where(irow == 3, g2, 0.0))))


def moe_route(x_parts, oa, ob, wa, wb, g, router_t):
    n, d = oa.shape[0], wa.shape[1]
    ne = router_t.shape[0]
    assert ne == 8 and n % MOE_TT == 0
    nt = n // MOE_TT

    def row_tile(i):
        return jnp.minimum(i, nt - 1)

    mixer_specs, first = _mixer_specs(x_parts, oa, ob, wa, wb, MOE_TT, row_tile)
    return pl.pallas_call(
        functools.partial(_moe_route_kernel, n_x=len(x_parts), first=first, n_tiles=nt),
        grid=(nt + 1,),
        in_specs=mixer_specs + [pl.BlockSpec((1, d), lambda i: (0, 0)),
                                pl.BlockSpec((ne, d), lambda i: (0, 0))],
        out_specs=[pl.BlockSpec((MOE_TT, d), lambda i: (row_tile(i), 0)),
                   pl.BlockSpec((1, MOE_R, d), lambda i: (i, 0, 0)),
                   pl.BlockSpec((1, ne, LANES), lambda i: (i, 0, 0)),
                   pl.BlockSpec((1, 8, MOE_TT), lambda i: (i, 0, 0))],
        out_shape=[jax.ShapeDtypeStruct((n, d), F32),
                   jax.ShapeDtypeStruct((nt + 1, MOE_R, d), BF16),
                   jax.ShapeDtypeStruct((nt + 1, ne, LANES), F32),
                   jax.ShapeDtypeStruct((nt + 1, 8, MOE_TT), F32)],
        compiler_params=pltpu.CompilerParams(dimension_semantics=("arbitrary",),
                                             vmem_limit_bytes=VMEM_LIMIT),
        name="moe_route")(*x_parts, oa, ob, wa, wb, g, router_t)


def _moe_tables(padded_counts, tm):
    nt, ne = padded_counts.shape
    tile_ch = tm // MOE_CH
    lc = padded_counts // MOE_CH
    loc = jnp.cumsum(lc, axis=1) - lc
    per_expert = jnp.sum(lc, axis=0)
    per_expert_pad = (per_expert + tile_ch - 1) // tile_ch * tile_ch
    expert_end = jnp.cumsum(per_expert_pad)
    expert_base = expert_end - per_expert_pad
    seg_dst = expert_base[None, :] + jnp.cumsum(lc, axis=0) - lc
    max_chunks = (2 * nt * MOE_TT) // MOE_CH + nt * ne + ne * (tile_ch - 1)
    n_tiles = -(-max_chunks // tile_ch)
    n_active = (expert_end[-1] // tile_ch).astype(jnp.int32)
    tile_expert = jnp.minimum(jnp.sum(jnp.arange(n_tiles)[:, None] >= (expert_end // tile_ch)[None, :], axis=1),
                              ne - 1).astype(jnp.int32)
    start = seg_dst.reshape(-1)
    length = lc.reshape(-1)
    source = (jnp.arange(nt)[:, None] * MOE_LCH + loc).reshape(-1)
    d = jnp.arange(n_tiles * tile_ch)[:, None]
    hit = (d >= start[None, :]) & (d < (start + length)[None, :])
    zero_chunk = nt * MOE_LCH
    ffn_src = jnp.sum(jnp.where(hit, (source - start)[None, :] + d, 0), axis=1)
    ffn_src = jnp.where(jnp.any(hit, axis=1), ffn_src, zero_chunk).astype(jnp.int32)
    l = jnp.arange(MOE_LCH)[None, :, None]
    own = (l >= loc[:, None, :]) & (l < (loc + lc)[:, None, :])
    cmb = jnp.sum(jnp.where(own, (seg_dst - loc)[:, None, :] + l, 0), axis=2).astype(jnp.int32)
    n_valid = jnp.sum(lc, axis=1).astype(jnp.int32)
    return tile_expert, n_active.reshape(1), ffn_src, cmb.reshape(-1), n_valid, n_tiles


def _chunk_copies(table_ref, first, count, src_hbm, dst_ref, sem, start, offset=0):
    def one(c):
        row = pl.multiple_of(table_ref[first + c] * MOE_CH, MOE_CH)
        copy = pltpu.make_async_copy(src_hbm.at[pl.ds(row, MOE_CH), :],
                                     dst_ref.at[pl.ds(pl.multiple_of(c * MOE_CH, MOE_CH), MOE_CH), :], sem)
        if start:
            copy.start()
        else:
            copy.wait()

    group = 8 if isinstance(count, int) and count % 8 == 0 else 1

    def body(g, carry):
        for u in range(group):
            one(offset + g * group + u)
        return carry
    lax.fori_loop(0, count // group, body, 0)


def _moe_ffn_kernel(te_ref, nact_ref, src_ref, stage_hbm, w1_ref, w3_ref, w2_ref, o_ref,
                    xbuf, sem, acc_ref, *, tm):
    i = pl.program_id(0)
    j = pl.program_id(1)
    nf = pl.num_programs(1)
    nch = tm // MOE_CH
    nact = nact_ref[0]
    slot = i % 2

    @pl.when((i == 0) & (j == 0))
    def _prime():
        _chunk_copies(src_ref, 0, nch, stage_hbm, xbuf.at[0], sem.at[0], True)

    @pl.when((i < nact) & (j == 0))
    def _rotate():
        _chunk_copies(src_ref, i * nch, nch, stage_hbm, xbuf.at[slot], sem.at[slot], False)

        @pl.when(i + 1 < nact)
        def _prefetch():
            _chunk_copies(src_ref, (i + 1) * nch, nch, stage_hbm, xbuf.at[1 - slot], sem.at[1 - slot], True)

    @pl.when(i < nact)
    def _compute():
        @pl.when(j == 0)
        def _first():
            acc_ref[...] = jnp.zeros(acc_ref.shape, F32)

        x = xbuf[slot]
        tf = w1_ref.shape[2]
        sub = 2 * LANES if tf % (2 * LANES) == 0 else tf
        for c in range(tf // sub):
            cols = slice(c * sub, (c + 1) * sub)
            act = _swiglu_act(jnp.dot(x, w1_ref[0, :, cols], preferred_element_type=F32),
                              jnp.dot(x, w3_ref[0, :, cols], preferred_element_type=F32))
            acc_ref[...] += jnp.dot(act, w2_ref[0, cols, :], preferred_element_type=F32)

        @pl.when(j == nf - 1)
        def _store():
            o_ref[...] = acc_ref[...].astype(BF16)

    @pl.when((i >= nact) & (j == nf - 1))
    def _unused_tile():
        o_ref[...] = jnp.zeros(o_ref.shape, BF16)


def moe_ffn(stage, tile_expert, n_active, ffn_src, w1, w3, w2, n_tiles, tm):
    d = stage.shape[-1]
    f = w1.shape[2]
    tf = next((c for c in (14 * LANES, 4 * LANES) if f % c == 0), f)
    nf = f // tf

    def live(i, na):
        return jnp.minimum(i, na[0] - 1)

    def fcol(i, j, na):
        return jnp.where(i < na[0], j, nf - 1)

    grid_spec = pltpu.PrefetchScalarGridSpec(
        num_scalar_prefetch=3,
        grid=(n_tiles, nf),
        in_specs=[pl.BlockSpec(memory_space=pl.ANY),
                  pl.BlockSpec((1, d, tf), lambda i, j, te, na, src: (te[live(i, na)], 0, fcol(i, j, na))),
                  pl.BlockSpec((1, d, tf), lambda i, j, te, na, src: (te[live(i, na)], 0, fcol(i, j, na))),
                  pl.BlockSpec((1, tf, d), lambda i, j, te, na, src: (te[live(i, na)], fcol(i, j, na), 0))],
        out_specs=pl.BlockSpec((tm, d), lambda i, j, te, na, src: (i, 0)),
        scratch_shapes=[pltpu.VMEM((2, tm, d), BF16),
                        pltpu.SemaphoreType.DMA((2,)),
                        pltpu.VMEM((tm, d), F32)])
    return pl.pallas_call(
        functools.partial(_moe_ffn_kernel, tm=tm),
        grid_spec=grid_spec,
        out_shape=jax.ShapeDtypeStruct((n_tiles * tm, d), BF16),
        compiler_params=pltpu.CompilerParams(dimension_semantics=("arbitrary", "arbitrary"),
                                             vmem_limit_bytes=VMEM_LIMIT),
        name="moe_ffn")(tile_expert, n_active, ffn_src, stage.reshape(-1, d), w1, w3, w2)


def _moe_combine_kernel(cmb_ref, nval_ref, x_ref, info_ref, ys_hbm, gf_ref, *refs, first, final_norm):
    o_refs, (ybuf, sem) = refs[:-2], refs[-2:]
    t = pl.program_id(0)
    nt = pl.num_programs(0)
    slot = t % 2

    def fetch(tile, slot, start):
        nv = nval_ref[tile]
        sure = 2 * MOE_TT // MOE_CH
        _chunk_copies(cmb_ref, tile * MOE_LCH, sure, ys_hbm, ybuf.at[slot], sem.at[slot], start)
        _chunk_copies(cmb_ref, tile * MOE_LCH, nv - sure, ys_hbm, ybuf.at[slot], sem.at[slot], start, sure)
        if start:
            def clear(l, carry):
                ybuf[slot, pl.ds(pl.multiple_of(l * MOE_CH, MOE_CH), MOE_CH), :] = jnp.zeros(
                    (MOE_CH, ybuf.shape[2]), BF16)
                return carry
            lax.fori_loop(nv, MOE_LCH, clear, 0)

    @pl.when(t == 0)
    def _prime():
        fetch(0, 0, True)

    fetch(t, slot, False)

    @pl.when(t + 1 < nt)
    def _prefetch():
        fetch(t + 1, 1 - slot, True)

    info = info_ref[...]
    tt = info.shape[0]
    col = lax.broadcasted_iota(jnp.int32, (tt, MOE_R), 1).astype(F32)
    pick = jnp.concatenate([jnp.where(col == info[:, 0:1], 1.0, 0.0).astype(BF16),
                            jnp.where(col == info[:, 1:2], 1.0, 0.0).astype(BF16)], axis=0)
    y = jnp.dot(pick, ybuf[slot], preferred_element_type=F32)
    out = x_ref[...] + info[:, 2:3] * y[:tt] + info[:, 3:4] * y[tt:]
    out = _rms(out, gf_ref[...]) if final_norm else out
    if len(o_refs) == 1:
        o_refs[0][...] = out
    else:
        @pl.when(t < first)
        def _head():
            o_refs[0][...] = out

        @pl.when(t >= first)
        def _tail():
            o_refs[1][...] = out


def moe_combine(x, info_cols, ys, cmb, n_valid, g_final, final_norm, row_split):
    n, d = x.shape
    nt = n // MOE_TT
    assert sum(row_split) == n and all(r % MOE_TT == 0 for r in row_split) and len(row_split) <= 2
    first = row_split[0] // MOE_TT
    if len(row_split) == 1:
        out_specs = [pl.BlockSpec((MOE_TT, d), lambda t, c, v: (t, 0))]
    else:
        out_specs = [pl.BlockSpec((MOE_TT, d), lambda t, c, v: (jnp.minimum(t, first - 1), 0)),
                     pl.BlockSpec((MOE_TT, d), lambda t, c, v: (jnp.maximum(t - first, 0), 0))]
    grid_spec = pltpu.PrefetchScalarGridSpec(
        num_scalar_prefetch=2,
        grid=(nt,),
        in_specs=[pl.BlockSpec((MOE_TT, d), lambda t, c, v: (t, 0)),
                  pl.BlockSpec((MOE_TT, 8), lambda t, c, v: (t, 0)),
                  pl.BlockSpec(memory_space=pl.ANY),
                  pl.BlockSpec((1, d), lambda t, c, v: (0, 0))],
        out_specs=out_specs,
        scratch_shapes=[pltpu.VMEM((2, MOE_R, d), BF16),
                        pltpu.SemaphoreType.DMA((2,))])
    return pl.pallas_call(
        functools.partial(_moe_combine_kernel, first=first, final_norm=final_norm),
        grid_spec=grid_spec,
        out_shape=[jax.ShapeDtypeStruct((r, d), F32) for r in row_split],
        compiler_params=pltpu.CompilerParams(dimension_semantics=("arbitrary",),
                                             vmem_limit_bytes=VMEM_LIMIT),
        name="moe_combine")(cmb, n_valid, x, info_cols, ys, g_final)


def moe_top2(x_parts, oa, ob, wa, wb, g, router, w1, w3, w2, g_final, final_norm, row_split, tm=1024):
    n, d = oa.shape[0], wa.shape[1]
    nt = n // MOE_TT
    x, stage, counts, info = moe_route(x_parts, oa, ob, wa, wb, g, router.T.astype(BF16))
    padded_counts = counts[:nt, :, 0].astype(jnp.int32)
    tile_expert, n_active, ffn_src, cmb, n_valid, n_tiles = _moe_tables(padded_counts, tm)
    ys = moe_ffn(stage, tile_expert, n_active, ffn_src, w1, w3, w2, n_tiles, tm)
    info_cols = info[:nt].transpose(0, 2, 1).reshape(n, 8)
    return moe_combine(x, info_cols, ys, cmb, n_valid, g_final, final_norm, row_split)


def _trunk(x_parts, w_in, w_out, norm_mix, norm_ffn, lambda_qk, subln_g, na_rpb,
           ffn_w1, ffn_w3, ffn_w2, moe_router, moe_w1, moe_w3, moe_w2, norm_final):
    s, d = x_parts[0].shape[1:]
    rows = tuple(p.shape[0] * s for p in x_parts)
    b, n = sum(p.shape[0] for p in x_parts), sum(rows)
    depth = w_in.shape[0]
    diff_w = N_DIFF_HEADS * 2 * HEAD_DIM
    na_w = N_NA_HEADS * HEAD_DIM
    scale = HEAD_DIM ** -0.5
    col_scale = jnp.ones((w_in.shape[2],), F32)
    col_scale = col_scale.at[:diff_w].set(scale).at[3 * diff_w:3 * diff_w + na_w].set(scale)
    xs = [p.reshape(-1, d) for p in x_parts]
    for l in range(depth):
        proj = norm_proj(xs, norm_mix[l].reshape(1, d), (w_in[l] * col_scale).astype(BF16))
        proj = proj.reshape(b, s, -1)
        oa = diff_attn(proj, lambda_qk[l], subln_g[l], l).reshape(n, diff_w)
        ob = na_attn(proj, na_rpb[l], 3 * diff_w).reshape(n, na_w)
        wo = w_out[l].astype(BF16)
        mixed = (xs, oa, ob, wo[:diff_w], wo[diff_w:])
        last = l == depth - 1
        i = l // 2
        if l % 2 == 0:
            xs = [ffn_dense(out_proj(*mixed), norm_ffn[l].reshape(1, d), ffn_w1[i].astype(BF16),
                            ffn_w3[i].astype(BF16), ffn_w2[i].astype(BF16),
                            norm_final.reshape(1, d), final_norm=last)]
        else:
            xs = moe_top2(*mixed, norm_ffn[l].reshape(1, d), moe_router[i], moe_w1[i].astype(BF16),
                          moe_w3[i].astype(BF16), moe_w2[i].astype(BF16),
                          norm_final.reshape(1, d), final_norm=last,
                          row_split=rows if last else (n,))
    if len(xs) != len(rows):
        bounds = np.cumsum((0,) + rows)
        xs = [xs[0][lo:hi] for lo, hi in zip(bounds[:-1], bounds[1:])]
    return tuple(y.reshape(-1, s, d) for y in xs)


def kernel(x_prompt, x_sample, w_in, w_out, norm_mix, norm_ffn, lambda_qk, subln_g, na_rpb,
           ffn_w1, ffn_w3, ffn_w2, moe_router, moe_w1, moe_w3, moe_w2, norm_final):
    assert x_prompt.shape[1:] == x_sample.shape[1:]
    return _trunk([x_prompt, x_sample], w_in, w_out, norm_mix, norm_ffn, lambda_qk, subln_g, na_rpb,
                  ffn_w1, ffn_w3, ffn_w2, moe_router, moe_w1, moe_w3, moe_w2, norm_final)
```
